```python
import math
import jax
import jax.numpy as jnp
from jax import lax
import numpy as np

D_MODEL = 1024
BATCH = 8
SEQ = 4096
DEPTH = 2
DEC_BATCH = 32
DEC_SEQ = 4
PAST_LEN = 16384
PAGE_SIZE = 128

LRU_WIDTH = 768
LRU_BLOCKS = 12
LRU_BLOCK = LRU_WIDTH // LRU_BLOCKS
CONV_W = 4
RG_C = 8.0
SSM_WIDTH = 512
SSM_GROUP = 16
SSM_GROUPS = SSM_WIDTH // SSM_GROUP
SSM_STATE = 64
ATTN_HEADS = 12
HEAD_DIM = 64
ATTN_WIDTH = ATTN_HEADS * HEAD_DIM
DILATED_GROUPS = ((128, 1), (512, 4), (2048, 16))
W_MAX = max(w for w, _ in DILATED_GROUPS)
Q_BLOCK = 128
N_BRANCH = 3
D_FF = 4 * D_MODEL
IN_COLS = LRU_WIDTH + SSM_WIDTH + 3 * ATTN_WIDTH + N_BRANCH * D_MODEL
SPLITS = (LRU_WIDTH, LRU_WIDTH + SSM_WIDTH, LRU_WIDTH + SSM_WIDTH + ATTN_WIDTH,
          LRU_WIDTH + SSM_WIDTH + 2 * ATTN_WIDTH, LRU_WIDTH + SSM_WIDTH + 3 * ATTN_WIDTH)
RMS_EPS = 1e-6
NEG_INF = -1e30

kernel_name = 'hybrid_rglru_s5_dilated_attn_decode_step'


def rms_norm(x, g):
    xf = x.astype(jnp.float32)
    y = xf * lax.rsqrt(jnp.mean(xf * xf, axis=-1, keepdims=True) + RMS_EPS)
    return (y * g.astype(jnp.float32)).astype(x.dtype)


def _linear_combine(e1, e2):
    a1, b1 = e1
    a2, b2 = e2
    return a1 * a2, a2 * b1 + b2


def _complex_combine(e1, e2):
    ar1, ai1, br1, bi1 = e1
    ar2, ai2, br2, bi2 = e2
    return (ar1 * ar2 - ai1 * ai2, ar1 * ai2 + ai1 * ar2,
            ar2 * br1 - ai2 * bi1 + br2, ar2 * bi1 + ai2 * br1 + bi2)


def rglru_branch(xa, conv_state, h0, conv_w, conv_b, wr, br, wi, bi, lam):
    B, T, _ = xa.shape
    xe = jnp.concatenate([conv_state.astype(xa.dtype), xa], axis=1)
    xc = conv_b.astype(jnp.float32) + sum(conv_w[i].astype(jnp.float32) * xe[:, i:i + T].astype(jnp.float32)
                                          for i in range(CONV_W))
    new_conv = xe[:, T:]
    xb = xc.reshape(B, T, LRU_BLOCKS, LRU_BLOCK)
    r = jax.nn.sigmoid(jnp.einsum('btnc,ncd->btnd', xb, wr.astype(jnp.float32)) + br).reshape(B, T, LRU_WIDTH)
    i_g = jax.nn.sigmoid(jnp.einsum('btnc,ncd->btnd', xb, wi.astype(jnp.float32)) + bi).reshape(B, T, LRU_WIDTH)
    log_a = -RG_C * jax.nn.softplus(-lam.astype(jnp.float32)) * r
    a = jnp.exp(log_a)
    mult = jnp.sqrt(jnp.maximum(-jnp.expm1(2.0 * log_a), 0.0))
    b = mult * (i_g * xc)
    b = b.at[:, 0].add(a[:, 0] * h0.astype(jnp.float32))
    _, h = lax.associative_scan(_linear_combine, (a, b), axis=1)
    return h.astype(xa.dtype), new_conv, h[:, -1]


def s5_branch(u, s0_re, s0_im, lam_re, lam_im, log_dt, b_re, b_im, c_re, c_im, d_skip, glu_w, glu_b):
    B, T, _ = u.shape
    ug = u.astype(jnp.float32).reshape(B, T, SSM_GROUPS, SSM_GROUP)
    lr = lam_re.astype(jnp.float32)
    li = lam_im.astype(jnp.float32)
    dt = jnp.exp(log_dt.astype(jnp.float32))[:, None]
    mag = jnp.exp(lr * dt)
    ang = li * dt
    ab_re = mag * jnp.cos(ang)
    ab_im = mag * jnp.sin(ang)
    den = lr * lr + li * li
    f_re = ((ab_re - 1.0) * lr + ab_im * li) / den
    f_im = (ab_im * lr - (ab_re - 1.0) * li) / den
    br_ = b_re.astype(jnp.float32)
    bi_ = b_im.astype(jnp.float32)
    bb_re = f_re[..., None] * br_ - f_im[..., None] * bi_
    bb_im = f_re[..., None] * bi_ + f_im[..., None] * br_
    bu_re = jnp.einsum('btgc,gnc->btgn', ug, bb_re)
    bu_im = jnp.einsum('btgc,gnc->btgn', ug, bb_im)
    s0r = s0_re.astype(jnp.float32)
    s0i = s0_im.astype(jnp.float32)
    bu_re = bu_re.at[:, 0].add(ab_re * s0r - ab_im * s0i)
    bu_im = bu_im.at[:, 0].add(ab_re * s0i + ab_im * s0r)
    a_re = jnp.broadcast_to(ab_re, (1, T) + ab_re.shape)
    a_im = jnp.broadcast_to(ab_im, (1, T) + ab_im.shape)
    _, _, s_re, s_im = lax.associative_scan(_complex_combine, (a_re, a_im, bu_re, bu_im), axis=1)
    y = (jnp.einsum('btgn,gcn->btgc', s_re, c_re.astype(jnp.float32))
         - jnp.einsum('btgn,gcn->btgc', s_im, c_im.astype(jnp.float32))
         + d_skip.astype(jnp.float32) * ug)
    z = jax.nn.gelu(y.reshape(B, T, SSM_WIDTH))
    gl = z @ glu_w.astype(jnp.float32) + glu_b.astype(jnp.float32)
    out = gl[..., :SSM_WIDTH] * jax.nn.sigmoid(gl[..., SSM_WIDTH:])
    return out.astype(u.dtype), s_re[:, -1], s_im[:, -1]


def _from_sub(a, T):
    B, d, nb, qb = a.shape[:4]
    rest = a.shape[4:]
    a = a.reshape((B, d, nb * qb) + rest)
    a = jnp.moveaxis(a, 1, 2)
    return a.reshape((B, nb * qb * d) + rest)[:, :T]


def dilated_window_attention(q, k_new, v_new, k_past, v_past):
    B, T, H, Dh = q.shape
    P = k_past.shape[1]
    front = jnp.zeros((B, W_MAX - P, H, Dh), q.dtype)
    k_all = jnp.concatenate([front, k_past.astype(q.dtype), k_new], axis=1)
    v_all = jnp.concatenate([front, v_past.astype(q.dtype), v_new], axis=1)
    scale = HEAD_DIM ** -0.5
    outs, maxes, dens = [], [], []
    for w, d in DILATED_GROUPS:
        n_back = w // d
        n_q = -(-T // d)
        qb = min(Q_BLOCK, n_q)
        n_blk = -(-n_q // qb)
        n_q_pad = n_blk * qb
        t_pad = n_q_pad * d - T
        qg = jnp.pad(q, ((0, 0), (0, t_pad), (0, 0), (0, 0))).reshape(B, n_q_pad, d, H, Dh)
        qg = jnp.moveaxis(qg, 2, 1).reshape(B, d, n_blk, qb, H, Dh)

        def to_sub(a):
            a = jnp.pad(a[:, W_MAX - w:], ((0, 0), (0, t_pad), (0, 0), (0, 0)))
            a = a.reshape(B, n_back + n_q_pad, d, H, Dh)
            return jnp.moveaxis(a, 2, 1)

        idx = jnp.arange(n_blk)[:, None] * qb + jnp.arange(qb + n_back)[None, :]
        kb = to_sub(k_all)[:, :, idx]
        vb = to_sub(v_all)[:, :, idx]
        s = jnp.einsum('brnqhc,brnkhc->brnhqk', qg, kb, preferred_element_type=jnp.float32) * scale
        qq = jnp.arange(qb)[:, None]
        kk = jnp.arange(qb + n_back)[None, :]
        band = (kk >= qq) & (kk <= qq + n_back)
        res = jnp.arange(d)[:, None, None]
        key_ok = idx[None] * d + res >= w - min(w, P)
        mask = band[None, None, None] & key_ok[:, :, None, None, :]
        s = jnp.where(mask[None], s, NEG_INF)
        m = jnp.max(s, axis=-1, keepdims=True)
        p = jnp.exp(s - m)
        den = jnp.sum(p, axis=-1)
        den_t = jnp.moveaxis(den, -1, -2)
        o = jnp.einsum('brnhqk,brnkhc->brnqhc', p, vb.astype(jnp.float32)) / den_t[..., None]
        outs.append(_from_sub(o, T))
        maxes.append(_from_sub(jnp.moveaxis(m[..., 0], -1, -2), T))
        dens.append(_from_sub(den_t, T))
    m_all = jnp.stack(maxes, axis=0)
    m_top = jnp.max(m_all, axis=0, keepdims=True)
    alpha = jnp.stack(dens, axis=0) * jnp.exp(m_all - m_top)
    o_all = jnp.stack(outs, axis=0)
    out = jnp.sum(alpha[..., None] * o_all, axis=0) / jnp.sum(alpha, axis=0)[..., None]
    return out.astype(q.dtype)


def decoder_layer(x, conv_state, lru_state, ssm_re, ssm_im, k_past, v_past, lp):
    B, T, _ = x.shape
    h = rms_norm(x, lp['norm_pre_mix'])
    proj = h @ lp['w_in']
    xa, u, q, k, v, gates = jnp.split(proj, SPLITS, axis=-1)
    q = q.reshape(B, T, ATTN_HEADS, HEAD_DIM)
    k = k.reshape(B, T, ATTN_HEADS, HEAD_DIM)
    v = v.reshape(B, T, ATTN_HEADS, HEAD_DIM)
    ya, new_conv, new_lru = rglru_branch(xa, conv_state, lru_state, lp['conv_w'], lp['conv_b'],
                                         lp['lru_wr'], lp['lru_br'], lp['lru_wi'], lp['lru_bi'], lp['lru_lambda'])
    yb, new_re, new_im = s5_branch(u, ssm_re, ssm_im, lp['ssm_lambda_re'], lp['ssm_lambda_im'], lp['ssm_log_dt'],
                                   lp['ssm_b_re'], lp['ssm_b_im'], lp['ssm_c_re'], lp['ssm_c_im'], lp['ssm_d'],
                                   lp['glu_w'], lp['glu_b'])
    yc = dilated_window_attention(q, k, v, k_past, v_past).reshape(B, T, ATTN_WIDTH)
    g = jax.nn.sigmoid(gates.astype(jnp.float32)).reshape(B, T, N_BRANCH, D_MODEL)
    merged = (g[:, :, 0] * (ya @ lp['w_proj_a']).astype(jnp.float32)
              + g[:, :, 1] * (yb @ lp['w_proj_b']).astype(jnp.float32)
              + g[:, :, 2] * (yc @ lp['w_proj_c']).astype(jnp.float32))
    x = x + rms_norm(merged.astype(x.dtype) @ lp['w_out'], lp['norm_post_mix'])
    h2 = rms_norm(x, lp['norm_pre_mlp'])
    ff = jnp.square(jax.nn.relu(h2 @ lp['mlp_w1'])) @ lp['mlp_w2']
    x = x + rms_norm(ff, lp['norm_post_mlp'])
    return x, new_conv, new_lru, new_re, new_im, k, v


def setup_inputs(seed: int = 0) -> dict:
    key = jax.random.key(seed)
    ks = jax.random.split(key, 40)
    f32 = jnp.float32

    def nrm(k, shape, scale):
        return jax.random.normal(k, shape, f32) * scale

    w_buf = min(W_MAX, PAST_LEN)
    u_lru = jax.random.uniform(ks[20], (DEPTH, LRU_WIDTH), f32, 0.9, 0.999)
    p_lru = u_lru ** (1.0 / RG_C)
    lru_lambda = jnp.log(p_lru) - jnp.log1p(-p_lru)
    lam_im_base = math.pi * jnp.arange(SSM_STATE, dtype=f32)
    return {
        'x_prompt': nrm(ks[0], (BATCH, SEQ, D_MODEL), 1.0),
        'x_sample': nrm(ks[1], (DEC_BATCH, DEC_SEQ, D_MODEL), 1.0),
        'state_conv': nrm(ks[2], (DEPTH, DEC_BATCH, CONV_W - 1, LRU_WIDTH), 1.0),
        'state_lru': nrm(ks[3], (DEPTH, DEC_BATCH, LRU_WIDTH), 0.5),
        'state_ssm_re': nrm(ks[4], (DEPTH, DEC_BATCH, SSM_GROUPS, SSM_STATE), 0.1),
        'state_ssm_im': nrm(ks[5], (DEPTH, DEC_BATCH, SSM_GROUPS, SSM_STATE), 0.1),
        'cache_k': nrm(ks[6], (DEPTH, DEC_BATCH, w_buf, ATTN_HEADS, HEAD_DIM), 1.0),
        'cache_v': nrm(ks[7], (DEPTH, DEC_BATCH, w_buf, ATTN_HEADS, HEAD_DIM), 1.0),
        'norm_pre_mix': 1.0 + nrm(ks[8], (DEPTH, D_MODEL), 0.02),
        'norm_post_mix': 1.0 + nrm(ks[9], (DEPTH, D_MODEL), 0.02),
        'norm_pre_mlp': 1.0 + nrm(ks[10], (DEPTH, D_MODEL), 0.02),
        'norm_post_mlp': 1.0 + nrm(ks[11], (DEPTH, D_MODEL), 0.02),
        'w_in': nrm(ks[12], (DEPTH, D_MODEL, IN_COLS), D_MODEL ** -0.5),
        'conv_w': nrm(ks[13], (DEPTH, CONV_W, LRU_WIDTH), CONV_W ** -0.5),
        'conv_b': nrm(ks[14], (DEPTH, LRU_WIDTH), 0.01),
        'lru_wr': nrm(ks[15], (DEPTH, LRU_BLOCKS, LRU_BLOCK, LRU_BLOCK), LRU_BLOCK ** -0.5),
        'lru_br': nrm(ks[16], (DEPTH, LRU_BLOCKS, LRU_BLOCK), 0.01),
        'lru_wi': nrm(ks[17], (DEPTH, LRU_BLOCKS, LRU_BLOCK, LRU_BLOCK), LRU_BLOCK ** -0.5),
        'lru_bi': nrm(ks[18], (DEPTH, LRU_BLOCKS, LRU_BLOCK), 0.01),
        'lru_lambda': lru_lambda,
        'ssm_lambda_re': -0.5 + nrm(ks[21], (DEPTH, SSM_GROUPS, SSM_STATE), 0.01),
        'ssm_lambda_im': lam_im_base + nrm(ks[22], (DEPTH, SSM_GROUPS, SSM_STATE), 0.01),
        'ssm_log_dt': jax.random.uniform(ks[23], (DEPTH, SSM_GROUPS), f32, math.log(0.001), math.log(0.1)),
        'ssm_b_re': nrm(ks[24], (DEPTH, SSM_GROUPS, SSM_STATE, SSM_GROUP), (2 * SSM_GROUP) ** -0.5),
        'ssm_b_im': nrm(ks[25], (DEPTH, SSM_GROUPS, SSM_STATE, SSM_GROUP), (2 * SSM_GROUP) ** -0.5),
        'ssm_c_re': nrm(ks[26], (DEPTH, SSM_GROUPS, SSM_GROUP, SSM_STATE), (2 * SSM_STATE) ** -0.5),
        'ssm_c_im': nrm(ks[27], (DEPTH, SSM_GROUPS, SSM_GROUP, SSM_STATE), (2 * SSM_STATE) ** -0.5),
        'ssm_d': nrm(ks[28], (DEPTH, SSM_GROUPS, SSM_GROUP), 1.0),
        'glu_w': nrm(ks[29], (DEPTH, SSM_WIDTH, 2 * SSM_WIDTH), SSM_WIDTH ** -0.5),
        'glu_b': nrm(ks[30], (DEPTH, 2 * SSM_WIDTH), 0.01),
        'w_proj_a': nrm(ks[31], (DEPTH, LRU_WIDTH, D_MODEL), LRU_WIDTH ** -0.5),
        'w_proj_b': nrm(ks[32], (DEPTH, SSM_WIDTH, D_MODEL), SSM_WIDTH ** -0.5),
        'w_proj_c': nrm(ks[33], (DEPTH, ATTN_WIDTH, D_MODEL), ATTN_WIDTH ** -0.5),
        'w_out': nrm(ks[34], (DEPTH, D_MODEL, D_MODEL), D_MODEL ** -0.5),
        'mlp_w1': nrm(ks[35], (DEPTH, D_MODEL, D_FF), D_MODEL ** -0.5),
        'mlp_w2': nrm(ks[36], (DEPTH, D_FF, D_MODEL), D_FF ** -0.5),
    }


def reference(x_prompt, x_sample, state_conv, state_lru, state_ssm_re, state_ssm_im, cache_k, cache_v,
              norm_pre_mix, norm_post_mix, norm_pre_mlp, norm_post_mlp, w_in, conv_w, conv_b,
              lru_wr, lru_br, lru_wi, lru_bi, lru_lambda, ssm_lambda_re, ssm_lambda_im, ssm_log_dt,
              ssm_b_re, ssm_b_im, ssm_c_re, ssm_c_im, ssm_d, glu_w, glu_b,
              w_proj_a, w_proj_b, w_proj_c, w_out, mlp_w1, mlp_w2):
    Bp, Tp, _ = x_prompt.shape
    keep = min(W_MAX, Tp)
    xp = x_prompt
    xs = x_sample
    p_conv, p_lru, p_re, p_im, p_k, p_v = [], [], [], [], [], []
    s_conv, s_lru, s_re, s_im, s_k, s_v = [], [], [], [], [], []
    for l in range(DEPTH):
        lp = {
            'norm_pre_mix': norm_pre_mix[l], 'norm_post_mix': norm_post_mix[l],
            'norm_pre_mlp': norm_pre_mlp[l], 'norm_post_mlp': norm_post_mlp[l],
            'w_in': w_in[l], 'conv_w': conv_w[l], 'conv_b': conv_b[l],
            'lru_wr': lru_wr[l], 'lru_br': lru_br[l], 'lru_wi': lru_wi[l], 'lru_bi': lru_bi[l],
            'lru_lambda': lru_lambda[l],
            'ssm_lambda_re': ssm_lambda_re[l], 'ssm_lambda_im': ssm_lambda_im[l], 'ssm_log_dt': ssm_log_dt[l],
            'ssm_b_re': ssm_b_re[l], 'ssm_b_im': ssm_b_im[l], 'ssm_c_re': ssm_c_re[l], 'ssm_c_im': ssm_c_im[l],
            'ssm_d': ssm_d[l], 'glu_w': glu_w[l], 'glu_b': glu_b[l],
            'w_proj_a': w_proj_a[l], 'w_proj_b': w_proj_b[l], 'w_proj_c': w_proj_c[l], 'w_out': w_out[l],
            'mlp_w1': mlp_w1[l], 'mlp_w2': mlp_w2[l],
        }
        zc = jnp.zeros((Bp, CONV_W - 1, LRU_WIDTH), xp.dtype)
        zl = jnp.zeros((Bp, LRU_WIDTH), jnp.float32)
        zs = jnp.zeros((Bp, SSM_GROUPS, SSM_STATE), jnp.float32)
        zk = jnp.zeros((Bp, 0, ATTN_HEADS, HEAD_DIM), xp.dtype)
        xp, c_, h_, r_, i_, k_, v_ = decoder_layer(xp, zc, zl, zs, zs, zk, zk, lp)
        p_conv.append(c_); p_lru.append(h_); p_re.append(r_); p_im.append(i_)
        p_k.append(k_[:, Tp - keep:]); p_v.append(v_[:, Tp - keep:])
        xs, c_, h_, r_, i_, k_, v_ = decoder_layer(xs, state_conv[l], state_lru[l], state_ssm_re[l],
                                                   state_ssm_im[l], cache_k[l], cache_v[l], lp)
        s_conv.append(c_); s_lru.append(h_); s_re.append(r_); s_im.append(i_)
        s_k.append(k_); s_v.append(v_)
    return (xp, xs,
            jnp.stack(p_conv), jnp.stack(p_lru), jnp.stack(p_re), jnp.stack(p_im), jnp.stack(p_k), jnp.stack(p_v),
            jnp.stack(s_conv), jnp.stack(s_lru), jnp.stack(s_re), jnp.stack(s_im), jnp.stack(s_k), jnp.stack(s_v))
```

```python
import functools
import math

import jax
import jax.numpy as jnp
from jax import lax
from jax.experimental import pallas as pl
from jax.experimental.pallas import tpu as pltpu

F32 = jnp.float32
BF16 = jnp.bfloat16

RMS_EPS = 1e-6
NEG_INF = -1e30
RG_C = 8.0
ATTN_HEADS = 12
HEAD_DIM = 64
DILATED_GROUPS = ((128, 1), (512, 4), (2048, 16))
W_MAX = max(w for w, _ in DILATED_GROUPS)
Q_BLOCK = 128

LANES = 128
MXU_TILE = 256
VMEM_LIMIT = 56 * 1024 * 1024


def _cparams(sem):
    return pltpu.CompilerParams(dimension_semantics=sem, vmem_limit_bytes=VMEM_LIMIT)


def _rms(x, g):
    return x * lax.rsqrt(jnp.mean(x * x, axis=-1, keepdims=True) + RMS_EPS) * g


def _dot(a, b):
    return jnp.dot(a, b, preferred_element_type=F32)


def _expm1(x):
    e = jnp.exp(x)
    return jnp.where(jnp.abs(x) > 0.5, e - 1.0, jnp.tanh(0.5 * x) * (e + 1.0))


def _const_spec(shape):
    nd = len(shape)
    return pl.BlockSpec(shape, lambda *_: (0,) * nd)


def _in_proj_kernel(splits, x_ref, g_ref, w_ref, *out_refs):
    h = _rms(x_ref[...], g_ref[...]).astype(BF16)
    for (lo, hi), o_ref in zip(splits, out_refs):
        o_ref[...] = _dot(h, w_ref[:, lo:hi])


def _in_proj(x, g, w, widths, time_major, rows):
    B, T, D = x.shape
    splits, lo = [], 0
    for wd in widths:
        splits.append((lo, lo + wd))
        lo += wd
    out_shapes, out_specs = [], []
    for wd, tm in zip(widths, time_major):
        if tm:
            out_shapes.append(jax.ShapeDtypeStruct((T, B * wd), F32))
            out_specs.append(pl.BlockSpec((rows, wd), lambda b, i: (i, b)))
        else:
            out_shapes.append(jax.ShapeDtypeStruct((B, T, wd), F32))
            out_specs.append(pl.BlockSpec((None, rows, wd), lambda b, i: (b, i, 0)))
    outs = pl.pallas_call(
        functools.partial(_in_proj_kernel, tuple(splits)),
        grid=(B, T // rows),
        in_specs=[pl.BlockSpec((None, rows, D), lambda b, i: (b, i, 0)),
                  _const_spec((1, D)), _const_spec(w.shape)],
        out_specs=out_specs,
        out_shape=out_shapes,
        compiler_params=_cparams(("parallel", "parallel")),
        name="in_proj",
    )(x, g.reshape(1, D), w)
    return [o.reshape(T * B, wd) if tm else o for o, wd, tm in zip(outs, widths, time_major)]


def _rglru_kernel(nb, tt, xa_ref, cs_ref, h0_ref, cw_ref, cb_ref, wr_ref, br_ref, wi_ref, bi_ref, lam_ref,
                  ya_ref, nconv_ref, hlast_ref, xe_buf, a_buf, b_buf, h_carry):
    rows = tt * nb
    halo = cs_ref.shape[0]
    n_tap = cw_ref.shape[0]

    @pl.when(pl.program_id(0) == 0)
    def _():
        xe_buf[0:halo] = cs_ref[...]
        h_carry[...] = h0_ref[...]

    xe_buf[halo:halo + rows] = xa_ref[...]
    xc = cb_ref[...] + sum(cw_ref[i:i + 1, :] * xe_buf[i * nb:i * nb + rows] for i in range(n_tap))
    tail = xe_buf[rows:rows + halo]
    nconv_ref[...] = tail
    xe_buf[0:halo] = tail

    xcb = xc.astype(BF16)
    c = -RG_C * jax.nn.softplus(-lam_ref[...])
    for j in range(wr_ref.shape[0]):
        sl = slice(j * MXU_TILE, (j + 1) * MXU_TILE)
        r = jax.nn.sigmoid(_dot(xcb[:, sl], wr_ref[j]) + br_ref[:, sl])
        ig = jax.nn.sigmoid(_dot(xcb[:, sl], wi_ref[j]) + bi_ref[:, sl])
        log_a = c[:, sl] * r
        a_buf[:, sl] = jnp.exp(log_a)
        mult = jnp.sqrt(jnp.maximum(-_expm1(2.0 * log_a), 0.0))
        b_buf[:, sl] = mult * (ig * xc[:, sl])

    def step(t, h):
        rs = pl.ds(pl.multiple_of(t * nb, nb), nb)
        h = a_buf[rs, :] * h + b_buf[rs, :]
        ya_ref[rs, :] = h
        return h

    h = lax.fori_loop(0, tt, step, h_carry[...])
    h_carry[...] = h
    hlast_ref[...] = h


def _rglru(xa, conv_state, h0, p, nb, tt):
    n_rows, W = xa.shape
    T = n_rows // nb
    rows = tt * nb
    halo = conv_state.shape[0]
    assert rows >= halo and T % tt == 0
    consts = [p['conv_w'], p['conv_b'], p['wr'], p['br'], p['wi'], p['bi'], p['lam']]
    return pl.pallas_call(
        functools.partial(_rglru_kernel, nb, tt),
        grid=(T // tt,),
        in_specs=[pl.BlockSpec((rows, W), lambda i: (i, 0)), _const_spec(conv_state.shape), _const_spec(h0.shape)]
                 + [_const_spec(a.shape) for a in consts],
        out_specs=[pl.BlockSpec((rows, W), lambda i: (i, 0)), _const_spec(conv_state.shape), _const_spec(h0.shape)],
        out_shape=[jax.ShapeDtypeStruct((n_rows, W), F32), jax.ShapeDtypeStruct(conv_state.shape, F32),
                   jax.ShapeDtypeStruct(h0.shape, F32)],
        scratch_shapes=[pltpu.VMEM((rows + halo, W), F32), pltpu.VMEM((rows, W), F32),
                        pltpu.VMEM((rows, W), F32), pltpu.VMEM((nb, W), F32)],
        compiler_params=_cparams(("arbitrary",)),
        name="rglru",
    )(xa, conv_state, h0, *consts)


def _gelu_tanh(x):
    return 0.5 * x * (1.0 + jnp.tanh(math.sqrt(2.0 / math.pi) * (x + 0.044715 * (x * x * x))))


def _s5_kernel(nb, tt, chunk, u_ref, s0r_ref, s0i_ref, ar_ref, ai_ref, bre_ref, bim_ref, cre_ref, cim_ref,
               d_ref, gw_ref, gb_ref, yb_ref, lastr_ref, lasti_ref, sre, sim, car_r, car_i):
    n_state = sre.shape[1]
    n_half = bre_ref.shape[0]
    kw = bre_ref.shape[1]
    sw = bre_ref.shape[2]
    yw = cre_ref.shape[2]

    @pl.when(pl.program_id(0) == 0)
    def _():
        car_r[...] = s0r_ref[...]
        car_i[...] = s0i_ref[...]

    u = u_ref[...]
    ub = u.astype(BF16)
    for hf in range(n_half):
        sre[:, hf * sw:(hf + 1) * sw] = _dot(ub[:, hf * kw:(hf + 1) * kw], bre_ref[hf])
        sim[:, hf * sw:(hf + 1) * sw] = _dot(ub[:, hf * kw:(hf + 1) * kw], bim_ref[hf])

    for c0 in range(0, n_state, chunk):
        cs = slice(c0, c0 + chunk)
        ar = jnp.broadcast_to(ar_ref[:, cs], (nb, chunk))
        ai = jnp.broadcast_to(ai_ref[:, cs], (nb, chunk))

        def step(t, carry):
            s_r, s_i = carry
            rs = pl.ds(pl.multiple_of(t * nb, nb), nb)
            n_r = ar * s_r - ai * s_i + sre[rs, cs]
            n_i = ar * s_i + ai * s_r + sim[rs, cs]
            sre[rs, cs] = n_r
            sim[rs, cs] = n_i
            return n_r, n_i

        s_r, s_i = lax.fori_loop(0, tt, step, (car_r[:, cs], car_i[:, cs]))
        car_r[:, cs] = s_r
        car_i[:, cs] = s_i

    lastr_ref[...] = car_r[...]
    lasti_ref[...] = car_i[...]

    ys = []
    for hf in range(n_half):
        ss = slice(hf * sw, (hf + 1) * sw)
        ys.append(_dot(sre[:, ss].astype(BF16), cre_ref[hf]) - _dot(sim[:, ss].astype(BF16), cim_ref[hf]))
    y = jnp.concatenate(ys, axis=1) + d_ref[...] * u
    z = _gelu_tanh(y).astype(BF16)
    w_half = gw_ref.shape[1] // 2
    val = _dot(z, gw_ref[:, :w_half]) + gb_ref[:, :w_half]
    gate = _dot(z, gw_ref[:, w_half:]) + gb_ref[:, w_half:]
    yb_ref[...] = val * jax.nn.sigmoid(gate)


def _s5(u, s0r, s0i, p, nb, tt):
    n_rows, W = u.shape
    T = n_rows // nb
    rows = tt * nb
    n_state = s0r.shape[1]
    chunk = max(LANES, min(n_state, 4 * LANES * 8 // nb))
    consts = [p['a_re'], p['a_im'], p['b_re'], p['b_im'], p['c_re'], p['c_im'], p['d'], p['glu_w'], p['glu_b']]
    return pl.pallas_call(
        functools.partial(_s5_kernel, nb, tt, chunk),
        grid=(T // tt,),
        in_specs=[pl.BlockSpec((rows, W), lambda i: (i, 0)), _const_spec(s0r.shape), _const_spec(s0i.shape)]
                 + [_const_spec(a.shape) for a in consts],
        out_specs=[pl.BlockSpec((rows, W), lambda i: (i, 0)), _const_spec(s0r.shape), _const_spec(s0i.shape)],
        out_shape=[jax.ShapeDtypeStruct((n_rows, W), F32), jax.ShapeDtypeStruct(s0r.shape, F32),
                   jax.ShapeDtypeStruct(s0i.shape, F32)],
        scratch_shapes=[pltpu.VMEM((rows, n_state), F32), pltpu.VMEM((rows, n_state), F32),
                        pltpu.VMEM((nb, n_state), F32), pltpu.VMEM((nb, n_state), F32)],
        compiler_params=_cparams(("arbitrary",)),
        name="s5",
    )(u, s0r, s0i, *consts)


def _attn_prompt_kernel(tq, q_ref, kp_ref, kc_ref, vp_ref, vc_ref, o_ref, kall, vall, acc, mst, lst):
    it = pl.program_id(2)
    kall[0:tq] = kp_ref[...]
    kall[tq:2 * tq] = kc_ref[...]
    vall[0:tq] = vp_ref[...]
    vall[tq:2 * tq] = vc_ref[...]

    qb = Q_BLOCK
    scale = HEAD_DIM ** -0.5
    lane = lax.broadcasted_iota(jnp.int32, (1, LANES), 1)
    head0 = lane < HEAD_DIM
    qq = lax.broadcasted_iota(jnp.int32, (qb, 2 * qb), 0)
    kk = lax.broadcasted_iota(jnp.int32, (qb, 2 * qb), 1)
    band = (kk >= qq) & (kk <= qq + qb)
    band2 = jnp.concatenate([band, band], axis=0)
    kk2 = jnp.concatenate([kk, kk], axis=0)

    for g, (w, d) in enumerate(DILATED_GROUPS):
        assert w // d == qb
        nblk = tq // (qb * d)

        def body(idx, carry, g=g, d=d, nblk=nblk):
            r = idx // nblk
            blk = idx % nblk
            qstart = r + d * qb * blk
            kstart = tq + qstart - d * qb
            if d == 1:
                qs, ks = pl.ds(qstart, qb), pl.ds(kstart, 2 * qb)
            else:
                qs, ks = pl.ds(qstart, qb, stride=d), pl.ds(kstart, 2 * qb, stride=d)
            qt = q_ref[qs, :]
            kt = kall[ks, :].astype(BF16)
            vt = vall[ks, :].astype(BF16)
            q2 = jnp.concatenate([jnp.where(head0, qt, 0.0), jnp.where(head0, 0.0, qt)], axis=0).astype(BF16)
            s = lax.dot_general(q2, kt, (((1,), (1,)), ((), ())), preferred_element_type=F32) * scale
            first_key = it * tq + qstart - d * qb
            ok = band2 & (first_key + d * kk2 >= 0)
            s = jnp.where(ok, s, NEG_INF)
            m = jnp.max(s, axis=-1, keepdims=True)
            p = jnp.exp(s - m)
            l = jnp.sum(p, axis=-1, keepdims=True)
            pv = _dot(p.astype(BF16), vt)
            acc[g, qs, :] = jnp.where(head0, pv[0:qb], pv[qb:2 * qb])
            mst[g, qs, :] = jnp.where(head0, m[0:qb], m[qb:2 * qb])
            lst[g, qs, :] = jnp.where(head0, l[0:qb], l[qb:2 * qb])
            return carry

        lax.fori_loop(0, d * nblk, body, 0)

    n_g = len(DILATED_GROUPS)
    m_top = functools.reduce(jnp.maximum, [mst[g] for g in range(n_g)])
    num = 0.0
    den = 0.0
    for g in range(n_g):
        wgt = jnp.exp(mst[g] - m_top)
        num = num + wgt * acc[g]
        den = den + wgt * lst[g]
    o_ref[...] = num / den


def _attn_prompt(q, k, v, time_major_out):
    B, T, W = q.shape
    tq = W_MAX
    assert T % tq == 0 and W % LANES == 0
    n_hp = W // LANES
    cur = pl.BlockSpec((None, tq, LANES), lambda b, h, i: (b, i, h))
    prev = pl.BlockSpec((None, tq, LANES), lambda b, h, i: (b, jnp.maximum(i - 1, 0), h))
    if time_major_out:
        out_shape = jax.ShapeDtypeStruct((T, B * W), F32)
        out_spec = pl.BlockSpec((tq, LANES), lambda b, h, i: (i, b * n_hp + h))
    else:
        out_shape = jax.ShapeDtypeStruct((B, T, W), F32)
        out_spec = cur
    n_g = len(DILATED_GROUPS)
    return pl.pallas_call(
        functools.partial(_attn_prompt_kernel, tq),
        grid=(B, n_hp, T // tq),
        in_specs=[cur, prev, cur, prev, cur],
        out_specs=out_spec,
        out_shape=out_shape,
        scratch_shapes=[pltpu.VMEM((2 * tq, LANES), F32), pltpu.VMEM((2 * tq, LANES), F32),
                        pltpu.VMEM((n_g, tq, LANES), F32), pltpu.VMEM((n_g, tq, LANES), F32),
                        pltpu.VMEM((n_g, tq, LANES), F32)],
        compiler_params=_cparams(("parallel", "parallel", "arbitrary")),
        name="attn_prompt",
    )(q, k, k, v, v)


def _attn_sample_kernel(n_new, q_ref, kc_ref, vc_ref, kn_ref, vn_ref, o_ref):
    P = kc_ref.shape[0]
    W = kc_ref.shape[1]
    pad = q_ref.shape[0]
    n_rows = ATTN_HEADS * pad
    scale = HEAD_DIM ** -0.5
    q = q_ref[...]
    col_head = lax.broadcasted_iota(jnp.int32, (pad, W), 1) // HEAD_DIM
    qbd = jnp.concatenate([jnp.where(col_head == h, q, 0.0) for h in range(ATTN_HEADS)], axis=0).astype(BF16)

    def multiplicity(dist):
        cnt = jnp.zeros(dist.shape, F32)
        for w, d in DILATED_GROUPS:
            cnt = cnt + ((dist >= 0) & (dist <= w) & (dist % d == 0)).astype(F32)
        return cnt

    nt = (((1,), (1,)), ((), ()))
    s_c = lax.dot_general(qbd, kc_ref[...].astype(BF16), nt, preferred_element_type=F32) * scale
    s_n = lax.dot_general(qbd, kn_ref[...].astype(BF16), nt, preferred_element_type=F32) * scale
    j_c = lax.broadcasted_iota(jnp.int32, (n_rows, P), 0) % pad
    i_c = lax.broadcasted_iota(jnp.int32, (n_rows, P), 1)
    mult_c = multiplicity(j_c + P - i_c)
    j_n = lax.broadcasted_iota(jnp.int32, (n_rows, pad), 0) % pad
    i_n = lax.broadcasted_iota(jnp.int32, (n_rows, pad), 1)
    mult_n = jnp.where(i_n < n_new, multiplicity(j_n - i_n), 0.0)
    s_c = jnp.where(mult_c > 0, s_c, NEG_INF)
    s_n = jnp.where(mult_n > 0, s_n, NEG_INF)
    m = jnp.maximum(jnp.max(s_c, axis=-1, keepdims=True), jnp.max(s_n, axis=-1, keepdims=True))
    p_c = mult_c * jnp.exp(s_c - m)
    p_n = mult_n * jnp.exp(s_n - m)
    l = jnp.sum(p_c, axis=-1, keepdims=True) + jnp.sum(p_n, axis=-1, keepdims=True)
    pv = (_dot(p_c.astype(BF16), vc_ref[...].astype(BF16)) + _dot(p_n.astype(BF16), vn_ref[...].astype(BF16))) / l
    out = jnp.zeros((pad, W), F32)
    for h in range(ATTN_HEADS):
        out = out + jnp.where(col_head == h, pv[h * pad:(h + 1) * pad], 0.0)
    o_ref[...] = out


def _attn_sample(q, k_new, v_new, k_cache, v_cache, n_new):
    B, pad, W = q.shape
    P = k_cache.shape[1]
    small = pl.BlockSpec((None, pad, W), lambda b: (b, 0, 0))
    big = pl.BlockSpec((None, P, W), lambda b: (b, 0, 0))
    return pl.pallas_call(
        functools.partial(_attn_sample_kernel, n_new),
        grid=(B,),
        in_specs=[small, big, big, small, small],
        out_specs=small,
        out_shape=jax.ShapeDtypeStruct((B, pad, W), F32),
        compiler_params=_cparams(("parallel",)),
        name="attn_sample",
    )(q, k_cache, v_cache, k_new, v_new)


def _merge_kernel(x_ref, gpre_ref, wg_ref, ya_ref, yb_ref, yc_ref, wa_ref, wb_ref, wc_ref, wo_ref, gpost_ref, o_ref):
    x = x_ref[...]
    D = x.shape[1]
    h = _rms(x, gpre_ref[...]).astype(BF16)
    merged = None
    for i, (y_ref, w_ref) in enumerate(((ya_ref, wa_ref), (yb_ref, wb_ref), (yc_ref, wc_ref))):
        gate = jax.nn.sigmoid(_dot(h, wg_ref[:, i * D:(i + 1) * D]))
        term = gate * _dot(y_ref[...].astype(BF16), w_ref[...])
        merged = term if merged is None else merged + term
    t = _dot(merged.astype(BF16), wo_ref[...])
    o_ref[...] = x + _rms(t, gpost_ref[...])


def _merge(x, ya, yb, yc, p, rows):
    B, T, D = x.shape
    xspec = pl.BlockSpec((None, rows, D), lambda b, i: (b, i, 0))

    def tm(arr):
        wd = arr.shape[1]
        return arr.reshape(T, B * wd), pl.BlockSpec((rows, wd), lambda b, i: (i, b))

    (ya2, sa), (yb2, sb), (yc2, sc) = tm(ya), tm(yb), tm(yc)
    consts = [p['w_proj_a'], p['w_proj_b'], p['w_proj_c'], p['w_out'], p['norm_post_mix']]
    return pl.pallas_call(
        _merge_kernel,
        grid=(B, T // rows),
        in_specs=[xspec, _const_spec((1, D)), _const_spec(p['w_gates'].shape), sa, sb, sc]
                 + [_const_spec(a.shape) for a in consts],
        out_specs=xspec,
        out_shape=jax.ShapeDtypeStruct((B, T, D), F32),
        compiler_params=_cparams(("parallel", "parallel")),
        name="merge",
    )(x, p['norm_pre_mix'], p['w_gates'], ya2, yb2, yc2, *consts)


def _mlp_kernel(ff_chunk, x_ref, gpre_ref, w1_ref, w2_ref, gpost_ref, o_ref):
    x = x_ref[...]
    h = _rms(x, gpre_ref[...]).astype(BF16)
    acc = None
    for c0 in range(0, w1_ref.shape[1], ff_chunk):
        hid = jnp.square(jnp.maximum(_dot(h, w1_ref[:, c0:c0 + ff_chunk]), 0.0)).astype(BF16)
        part = _dot(hid, w2_ref[c0:c0 + ff_chunk, :])
        acc = part if acc is None else acc + part
    o_ref[...] = x + _rms(acc, gpost_ref[...])


def _mlp(x, p, rows):
    B, T, D = x.shape
    xspec = pl.BlockSpec((None, rows, D), lambda b, i: (b, i, 0))
    return pl.pallas_call(
        functools.partial(_mlp_kernel, 1024),
        grid=(B, T // rows),
        in_specs=[xspec, _const_spec((1, D)), _const_spec(p['mlp_w1'].shape), _const_spec(p['mlp_w2'].shape),
                  _const_spec((1, D))],
        out_specs=xspec,
        out_shape=jax.ShapeDtypeStruct((B, T, D), F32),
        compiler_params=_cparams(("parallel", "parallel")),
        name="mlp",
    )(x, p['norm_pre_mlp'], p['mlp_w1'], p['mlp_w2'], p['norm_post_mlp'])


def _block_diag_tiles(w, per_tile):
    n, c, d = w.shape
    w = w.reshape(n // per_tile, per_tile, c, d)
    eye = jnp.eye(per_tile, dtype=w.dtype)
    return jnp.einsum('jpcd,pq->jpcqd', w, eye).reshape(n // per_tile, per_tile * c, per_tile * d)


def _layer_params(l, a):
    D = a['w_in'].shape[1]
    lru_w = a['conv_w'].shape[2]
    n_blk, blk = a['lru_wr'].shape[1], a['lru_wr'].shape[2]
    G, N, GC = a['ssm_b_re'].shape[1:]
    ssm_w = G * GC
    attn_w = ATTN_HEADS * HEAD_DIM
    n_front = lru_w + ssm_w + 3 * attn_w
    row = lambda v: v.reshape(1, -1).astype(F32)
    p = {
        'norm_pre_mix': row(a['norm_pre_mix'][l]), 'norm_post_mix': row(a['norm_post_mix'][l]),
        'norm_pre_mlp': row(a['norm_pre_mlp'][l]), 'norm_post_mlp': row(a['norm_post_mlp'][l]),
        'w_front': a['w_in'][l][:, :n_front].astype(BF16), 'w_gates': a['w_in'][l][:, n_front:].astype(BF16),
        'w_proj_a': a['w_proj_a'][l].astype(BF16), 'w_proj_b': a['w_proj_b'][l].astype(BF16),
        'w_proj_c': a['w_proj_c'][l].astype(BF16), 'w_out': a['w_out'][l].astype(BF16),
        'mlp_w1': a['mlp_w1'][l].astype(BF16), 'mlp_w2': a['mlp_w2'][l].astype(BF16),
        'widths': (lru_w, ssm_w, attn_w, attn_w, attn_w),
    }
    per_tile = MXU_TILE // blk
    p['lru'] = {
        'conv_w': a['conv_w'][l].astype(F32), 'conv_b': row(a['conv_b'][l]),
        'wr': _block_diag_tiles(a['lru_wr'][l], per_tile).astype(BF16), 'br': row(a['lru_br'][l]),
        'wi': _block_diag_tiles(a['lru_wi'][l], per_tile).astype(BF16), 'bi': row(a['lru_bi'][l]),
        'lam': row(a['lru_lambda'][l]),
    }
    lr, li = a['ssm_lambda_re'][l].astype(F32), a['ssm_lambda_im'][l].astype(F32)
    dt = jnp.exp(a['ssm_log_dt'][l].astype(F32))[:, None]
    mag = jnp.exp(lr * dt)
    ang = li * dt
    ab_re, ab_im = mag * jnp.cos(ang), mag * jnp.sin(ang)
    den = lr * lr + li * li
    f_re = ((ab_re - 1.0) * lr + ab_im * li) / den
    f_im = (ab_im * lr - (ab_re - 1.0) * li) / den
    b_re, b_im = a['ssm_b_re'][l].astype(F32), a['ssm_b_im'][l].astype(F32)
    bb_re = f_re[..., None] * b_re - f_im[..., None] * b_im
    bb_im = f_re[..., None] * b_im + f_im[..., None] * b_re
    g_half = G // 2
    to_b = lambda m: _block_diag_tiles(jnp.swapaxes(m, 1, 2), g_half).astype(BF16)
    to_c = lambda m: _block_diag_tiles(jnp.swapaxes(m, 1, 2), g_half).astype(BF16)
    p['s5'] = {
        'a_re': ab_re.reshape(1, G * N), 'a_im': ab_im.reshape(1, G * N),
        'b_re': to_b(bb_re), 'b_im': to_b(bb_im),
        'c_re': to_c(a['ssm_c_re'][l].astype(F32)), 'c_im': to_c(a['ssm_c_im'][l].astype(F32)),
        'd': row(a['ssm_d'][l]), 'glu_w': a['glu_w'][l].astype(BF16), 'glu_b': row(a['glu_b'][l]),
    }
    return p


def _tm(state):
    nb, k, w = state.shape
    return jnp.swapaxes(state, 0, 1).reshape(k * nb, w)


def _layer_prompt(x, p, rows, tt):
    B, T, D = x.shape
    lru_w, ssm_w, attn_w = p['widths'][0], p['widths'][1], p['widths'][2]
    xa, u, q, k, v = _in_proj(x, p['norm_pre_mix'], p['w_front'], p['widths'],
                              (True, True, False, False, False), rows)
    n_state = p['s5']['a_re'].shape[1]
    halo = (p['lru']['conv_w'].shape[0] - 1) * B
    ya, nconv, hlast = _rglru(xa, jnp.zeros((halo, lru_w), F32), jnp.zeros((B, lru_w), F32), p['lru'], B, tt)
    yb, s_re, s_im = _s5(u, jnp.zeros((B, n_state), F32), jnp.zeros((B, n_state), F32), p['s5'], B, tt)
    yc = _attn_prompt(q, k, v, True).reshape(T * B, attn_w)
    x = _merge(x, ya, yb, yc, p, rows)
    x = _mlp(x, p, rows)
    return x, nconv, hlast, s_re, s_im, k, v


def _layer_sample(x, conv_state, h0, s0r, s0i, k_cache, v_cache, p, nb, n_new):
    rows = x.shape[1]
    lru_w, ssm_w, attn_w = p['widths'][0], p['widths'][1], p['widths'][2]
    xa, u, q, k, v = _in_proj(x, p['norm_pre_mix'], p['w_front'], p['widths'],
                              (True, True, False, False, False), rows)
    ya, nconv, hlast = _rglru(xa, conv_state, h0, p['lru'], nb, n_new)
    yb, s_re, s_im = _s5(u, s0r, s0i, p['s5'], nb, n_new)
    pad = 8

    def bm(arr):
        arr = jnp.swapaxes(arr.reshape(n_new, nb, attn_w), 0, 1)
        return jnp.pad(arr, ((0, 0), (0, pad - n_new), (0, 0)))

    yc = _attn_sample(bm(q), bm(k), bm(v), k_cache, v_cache, n_new)[:, :n_new]
    yc = jnp.swapaxes(yc, 0, 1).reshape(rows, attn_w)
    x = _merge(x, ya, yb, yc, p, rows)
    x = _mlp(x, p, rows)
    return x, nconv, hlast, s_re, s_im, k, v


def kernel(x_prompt, x_sample, state_conv, state_lru, state_ssm_re, state_ssm_im, cache_k, cache_v, norm_pre_mix, norm_post_mix, norm_pre_mlp, norm_post_mlp, w_in, conv_w, conv_b, lru_wr, lru_br, lru_wi, lru_bi, lru_lambda, ssm_lambda_re, ssm_lambda_im, ssm_log_dt, ssm_b_re, ssm_b_im, ssm_c_re, ssm_c_im, ssm_d, glu_w, glu_b, w_proj_a, w_proj_b, w_proj_c, w_out, mlp_w1, mlp_w2):
    a = dict(norm_pre_mix=norm_pre_mix, norm_post_mix=norm_post_mix, norm_pre_mlp=norm_pre_mlp,
             norm_post_mlp=norm_post_mlp, w_in=w_in, conv_w=conv_w, conv_b=conv_b, lru_wr=lru_wr, lru_br=lru_br,
             lru_wi=lru_wi, lru_bi=lru_bi, lru_lambda=lru_lambda, ssm_lambda_re=ssm_lambda_re,
             ssm_lambda_im=ssm_lambda_im, ssm_log_dt=ssm_log_dt, ssm_b_re=ssm_b_re, ssm_b_im=ssm_b_im,
             ssm_c_re=ssm_c_re, ssm_c_im=ssm_c_im, ssm_d=ssm_d, glu_w=glu_w, glu_b=glu_b, w_proj_a=w_proj_a,
             w_proj_b=w_proj_b, w_proj_c=w_proj_c, w_out=w_out, mlp_w1=mlp_w1, mlp_w2=mlp_w2)
    depth = w_in.shape[0]
    Bp, Tp, D = x_prompt.shape
    Bs, Ts, _ = x_sample.shape
    G, N = state_ssm_re.shape[2:]
    attn_w = ATTN_HEADS * HEAD_DIM
    P = cache_k.shape[2]
    assert P == W_MAX, "the window buffer must hold exactly the widest window"
    keep = min(W_MAX, Tp)
    conv_k = state_conv.shape[2]

    xp = x_prompt
    xs = jnp.swapaxes(x_sample, 0, 1).reshape(1, Ts * Bs, D)
    outs = [[] for _ in range(12)]
    for l in range(depth):
        p = _layer_params(l, a)
        xp, c_, h_, r_, i_, k_, v_ = _layer_prompt(xp, p, rows=512, tt=64)
        pk = k_[:, Tp - keep:].reshape(Bp, keep, ATTN_HEADS, HEAD_DIM)
        pv = v_[:, Tp - keep:].reshape(Bp, keep, ATTN_HEADS, HEAD_DIM)
        for lst, val in zip(outs[:6], (jnp.swapaxes(c_.reshape(conv_k, Bp, -1), 0, 1), h_,
                                       r_.reshape(Bp, G, N), i_.reshape(Bp, G, N), pk, pv)):
            lst.append(val)
        xs, c_, h_, r_, i_, k_, v_ = _layer_sample(
            xs, _tm(state_conv[l]), state_lru[l], state_ssm_re[l].reshape(Bs, G * N),
            state_ssm_im[l].reshape(Bs, G * N), cache_k[l].reshape(Bs, P, attn_w),
            cache_v[l].reshape(Bs, P, attn_w), p, Bs, Ts)
        sk = jnp.swapaxes(k_.reshape(Ts, Bs, ATTN_HEADS, HEAD_DIM), 0, 1)
        sv = jnp.swapaxes(v_.reshape(Ts, Bs, ATTN_HEADS, HEAD_DIM), 0, 1)
        for lst, val in zip(outs[6:], (jnp.swapaxes(c_.reshape(conv_k, Bs, -1), 0, 1), h_,
                                       r_.reshape(Bs, G, N), i_.reshape(Bs, G, N), sk, sv)):
            lst.append(val)
    ys = jnp.swapaxes(xs.reshape(Ts, Bs, D), 0, 1)
    return (xp, ys) + tuple(jnp.stack(o) for o in outs)
```

```python
import functools
import math

import jax
import jax.numpy as jnp
from jax import lax
from jax.experimental import pallas as pl
from jax.experimental.pallas import tpu as pltpu

F32 = jnp.float32
BF16 = jnp.bfloat16

RMS_EPS = 1e-6
NEG_INF = -1e30
RG_C = 8.0
ATTN_HEADS = 12
HEAD_DIM = 64
DILATED_GROUPS = ((128, 1), (512, 4), (2048, 16))
W_MAX = max(w for w, _ in DILATED_GROUPS)
Q_BLOCK = 128

LANES = 128
MXU_TILE = 256
VMEM_LIMIT = 56 * 1024 * 1024


def _cparams(sem):
    return pltpu.CompilerParams(dimension_semantics=sem, vmem_limit_bytes=VMEM_LIMIT)


def _rms(x, g):
    return x * lax.rsqrt(jnp.mean(x * x, axis=-1, keepdims=True) + RMS_EPS) * g


def _dot(a, b):
    return jnp.dot(a, b, preferred_element_type=F32)


def _expm1(x):
    e = jnp.exp(x)
    return jnp.where(jnp.abs(x) > 0.5, e - 1.0, jnp.tanh(0.5 * x) * (e + 1.0))


def _const_spec(shape):
    nd = len(shape)
    return pl.BlockSpec(shape, lambda *_: (0,) * nd)


def _in_proj_kernel(splits, x_ref, g_ref, w_ref, *out_refs):
    h = _rms(x_ref[...], g_ref[...]).astype(BF16)
    for (lo, hi), o_ref in zip(splits, out_refs):
        o_ref[...] = _dot(h, w_ref[:, lo:hi])


def _in_proj(x, g, w, widths, rows):
    B, T, D = x.shape
    splits, lo = [], 0
    for wd in widths:
        splits.append((lo, lo + wd))
        lo += wd
    return pl.pallas_call(
        functools.partial(_in_proj_kernel, tuple(splits)),
        grid=(B, T // rows),
        in_specs=[pl.BlockSpec((None, rows, D), lambda b, i: (b, i, 0)),
                  _const_spec((1, D)), _const_spec(w.shape)],
        out_specs=[pl.BlockSpec((None, rows, wd), lambda b, i: (b, i, 0)) for wd in widths],
        out_shape=[jax.ShapeDtypeStruct((B, T, wd), F32) for wd in widths],
        compiler_params=_cparams(("parallel", "parallel")),
        name="in_proj",
    )(x, g.reshape(1, D), w)


def _load_time_major(ref):
    if len(ref.shape) == 2:
        return ref[...]
    nb, tt, w = ref.shape
    return jnp.swapaxes(ref[...], 0, 1).reshape(tt * nb, w)


def _store_time_major(ref, val):
    if len(ref.shape) == 2:
        ref[...] = val
    else:
        nb, tt, w = ref.shape
        ref[...] = jnp.swapaxes(val.reshape(tt, nb, w), 0, 1)


def _seq_specs(x, nb, tt):
    w = x.shape[-1]
    if x.ndim == 3:
        return pl.BlockSpec((nb, tt, w), lambda i: (0, i, 0)), x.shape[1]
    return pl.BlockSpec((tt * nb, w), lambda i: (i, 0)), x.shape[0] // nb


def _rglru_kernel(nb, tt, xa_ref, cs_ref, h0_ref, cw_ref, cb_ref, wr_ref, br_ref, wi_ref, bi_ref, lam_ref,
                  ya_ref, nconv_ref, hlast_ref, xe_buf, a_buf, b_buf, h_carry):
    rows = tt * nb
    halo = cs_ref.shape[0]
    n_tap = cw_ref.shape[0]

    @pl.when(pl.program_id(0) == 0)
    def _():
        xe_buf[0:halo] = cs_ref[...]
        h_carry[...] = h0_ref[...]

    xe_buf[halo:halo + rows] = _load_time_major(xa_ref)
    xc = cb_ref[...] + sum(cw_ref[i:i + 1, :] * xe_buf[i * nb:i * nb + rows] for i in range(n_tap))
    tail = xe_buf[rows:rows + halo]
    nconv_ref[...] = tail
    xe_buf[0:halo] = tail

    xcb = xc.astype(BF16)
    c = -RG_C * jax.nn.softplus(-lam_ref[...])
    for j in range(wr_ref.shape[0]):
        sl = slice(j * MXU_TILE, (j + 1) * MXU_TILE)
        r = jax.nn.sigmoid(_dot(xcb[:, sl], wr_ref[j]) + br_ref[:, sl])
        ig = jax.nn.sigmoid(_dot(xcb[:, sl], wi_ref[j]) + bi_ref[:, sl])
        log_a = c[:, sl] * r
        a_buf[:, sl] = jnp.exp(log_a)
        mult = jnp.sqrt(jnp.maximum(-_expm1(2.0 * log_a), 0.0))
        b_buf[:, sl] = mult * (ig * xc[:, sl])

    def step(t, h):
        rs = pl.ds(pl.multiple_of(t * nb, nb), nb)
        h = a_buf[rs, :] * h + b_buf[rs, :]
        a_buf[rs, :] = h
        return h

    h = lax.fori_loop(0, tt, step, h_carry[...])
    h_carry[...] = h
    hlast_ref[...] = h
    _store_time_major(ya_ref, a_buf[...])


def _rglru(xa, conv_state, h0, p, nb, tt):
    W = xa.shape[-1]
    seq_spec, T = _seq_specs(xa, nb, tt)
    rows = tt * nb
    halo = conv_state.shape[0]
    assert rows >= halo and T % tt == 0
    consts = [p['conv_w'], p['conv_b'], p['wr'], p['br'], p['wi'], p['bi'], p['lam']]
    return pl.pallas_call(
        functools.partial(_rglru_kernel, nb, tt),
        grid=(T // tt,),
        in_specs=[seq_spec, _const_spec(conv_state.shape), _const_spec(h0.shape)]
                 + [_const_spec(a.shape) for a in consts],
        out_specs=[seq_spec, _const_spec(conv_state.shape), _const_spec(h0.shape)],
        out_shape=[jax.ShapeDtypeStruct(xa.shape, F32), jax.ShapeDtypeStruct(conv_state.shape, F32),
                   jax.ShapeDtypeStruct(h0.shape, F32)],
        scratch_shapes=[pltpu.VMEM((rows + halo, W), F32), pltpu.VMEM((rows, W), F32),
                        pltpu.VMEM((rows, W), F32), pltpu.VMEM((nb, W), F32)],
        compiler_params=_cparams(("arbitrary",)),
        name="rglru",
    )(xa, conv_state, h0, *consts)


def _gelu_tanh(x):
    return 0.5 * x * (1.0 + jnp.tanh(math.sqrt(2.0 / math.pi) * (x + 0.044715 * (x * x * x))))


def _s5_kernel(nb, tt, chunk, u_ref, s0r_ref, s0i_ref, ar_ref, ai_ref, bre_ref, bim_ref, cre_ref, cim_ref,
               d_ref, gw_ref, gb_ref, yb_ref, lastr_ref, lasti_ref, sre, sim, car_r, car_i):
    n_state = sre.shape[1]
    n_half = bre_ref.shape[0]
    kw = bre_ref.shape[1]
    sw = bre_ref.shape[2]
    yw = cre_ref.shape[2]

    @pl.when(pl.program_id(0) == 0)
    def _():
        car_r[...] = s0r_ref[...]
        car_i[...] = s0i_ref[...]

    u = _load_time_major(u_ref)
    ub = u.astype(BF16)
    for hf in range(n_half):
        sre[:, hf * sw:(hf + 1) * sw] = _dot(ub[:, hf * kw:(hf + 1) * kw], bre_ref[hf])
        sim[:, hf * sw:(hf + 1) * sw] = _dot(ub[:, hf * kw:(hf + 1) * kw], bim_ref[hf])

    for c0 in range(0, n_state, chunk):
        cs = slice(c0, c0 + chunk)
        ar = jnp.broadcast_to(ar_ref[:, cs], (nb, chunk))
        ai = jnp.broadcast_to(ai_ref[:, cs], (nb, chunk))

        def step(t, carry):
            s_r, s_i = carry
            rs = pl.ds(pl.multiple_of(t * nb, nb), nb)
            n_r = ar * s_r - ai * s_i + sre[rs, cs]
            n_i = ar * s_i + ai * s_r + sim[rs, cs]
            sre[rs, cs] = n_r
            sim[rs, cs] = n_i
            return n_r, n_i

        s_r, s_i = lax.fori_loop(0, tt, step, (car_r[:, cs], car_i[:, cs]))
        car_r[:, cs] = s_r
        car_i[:, cs] = s_i

    lastr_ref[...] = car_r[...]
    lasti_ref[...] = car_i[...]

    ys = []
    for hf in range(n_half):
        ss = slice(hf * sw, (hf + 1) * sw)
        ys.append(_dot(sre[:, ss].astype(BF16), cre_ref[hf]) - _dot(sim[:, ss].astype(BF16), cim_ref[hf]))
    y = jnp.concatenate(ys, axis=1) + d_ref[...] * u
    z = _gelu_tanh(y).astype(BF16)
    w_half = gw_ref.shape[1] // 2
    val = _dot(z, gw_ref[:, :w_half]) + gb_ref[:, :w_half]
    gate = _dot(z, gw_ref[:, w_half:]) + gb_ref[:, w_half:]
    _store_time_major(yb_ref, val * jax.nn.sigmoid(gate))


def _s5(u, s0r, s0i, p, nb, tt):
    seq_spec, T = _seq_specs(u, nb, tt)
    rows = tt * nb
    n_state = s0r.shape[1]
    chunk = max(LANES, min(n_state, 4 * LANES * 8 // nb))
    consts = [p['a_re'], p['a_im'], p['b_re'], p['b_im'], p['c_re'], p['c_im'], p['d'], p['glu_w'], p['glu_b']]
    return pl.pallas_call(
        functools.partial(_s5_kernel, nb, tt, chunk),
        grid=(T // tt,),
        in_specs=[seq_spec, _const_spec(s0r.shape), _const_spec(s0i.shape)]
                 + [_const_spec(a.shape) for a in consts],
        out_specs=[seq_spec, _const_spec(s0r.shape), _const_spec(s0i.shape)],
        out_shape=[jax.ShapeDtypeStruct(u.shape, F32), jax.ShapeDtypeStruct(s0r.shape, F32),
                   jax.ShapeDtypeStruct(s0i.shape, F32)],
        scratch_shapes=[pltpu.VMEM((rows, n_state), F32), pltpu.VMEM((rows, n_state), F32),
                        pltpu.VMEM((nb, n_state), F32), pltpu.VMEM((nb, n_state), F32)],
        compiler_params=_cparams(("arbitrary",)),
        name="s5",
    )(u, s0r, s0i, *consts)


def _attn_prompt_kernel(tq, q_ref, kp_ref, kc_ref, vp_ref, vc_ref, o_ref,
                        kall, vall, acc, mst, lst, bias, sbuf, pbuf, mb, lb):
    it = pl.program_id(2)
    kall[0:tq] = kp_ref[...]
    kall[tq:2 * tq] = kc_ref[...]
    vall[0:tq] = vp_ref[...]
    vall[tq:2 * tq] = vc_ref[...]

    qb = Q_BLOCK
    scale = HEAD_DIM ** -0.5
    lane = lax.broadcasted_iota(jnp.int32, (1, LANES), 1)
    head0 = lane < HEAD_DIM
    qq = lax.broadcasted_iota(jnp.int32, (2 * qb, 2 * qb), 0) % qb
    kk = lax.broadcasted_iota(jnp.int32, (2 * qb, 2 * qb), 1)
    band = (kk >= qq) & (kk <= qq + qb)
    bias[0] = jnp.where(band, 0.0, NEG_INF)
    bias[1] = jnp.where(band & (kk >= qb), 0.0, NEG_INF)

    for g, (w, d) in enumerate(DILATED_GROUPS):
        assert w // d == qb
        nblk = tq // (qb * d)

        def rows(idx, d=d, nblk=nblk):
            qstart = idx // nblk + d * qb * (idx % nblk)
            kstart = tq + qstart - d * qb
            if d == 1:
                return qstart, pl.ds(qstart, qb), pl.ds(kstart, 2 * qb)
            return qstart, pl.ds(qstart, qb, stride=d), pl.ds(kstart, 2 * qb, stride=d)

        def scores(idx, carry, d=d):
            qstart, qs, ks = rows(idx)
            qt = q_ref[qs, :] * scale
            kt = kall[ks, :].astype(BF16)
            q2 = jnp.concatenate([jnp.where(head0, qt, 0.0), jnp.where(head0, 0.0, qt)], axis=0).astype(BF16)
            s = lax.dot_general(q2, kt, (((1,), (1,)), ((), ())), preferred_element_type=F32)
            no_prev = (it * tq + qstart - d * qb < 0).astype(jnp.int32)
            s = s + bias[no_prev]
            sbuf[idx] = s
            mb[idx] = jnp.broadcast_to(jnp.max(s, axis=-1, keepdims=True), (2 * qb, LANES))
            return carry

        def probs(idx, carry):
            m = mb[idx]
            p_lo = jnp.exp(sbuf[idx, :, 0:LANES] - m)
            p_hi = jnp.exp(sbuf[idx, :, LANES:2 * LANES] - m)
            pbuf[idx, :, 0:LANES] = p_lo.astype(BF16)
            pbuf[idx, :, LANES:2 * LANES] = p_hi.astype(BF16)
            lb[idx] = jnp.broadcast_to(jnp.sum(p_lo + p_hi, axis=-1, keepdims=True), (2 * qb, LANES))
            return carry

        def values(idx, carry, g=g):
            _, qs, ks = rows(idx)
            pv = _dot(pbuf[idx], vall[ks, :].astype(BF16))
            acc[g, qs, :] = jnp.where(head0, pv[0:qb], pv[qb:2 * qb])
            mst[g, qs, :] = jnp.where(head0, mb[idx, 0:qb, :], mb[idx, qb:2 * qb, :])
            lst[g, qs, :] = jnp.where(head0, lb[idx, 0:qb, :], lb[idx, qb:2 * qb, :])
            return carry

        for phase in (scores, probs, values):
            lax.fori_loop(0, d * nblk, phase, 0, unroll=8)

    n_g = len(DILATED_GROUPS)
    m_top = functools.reduce(jnp.maximum, [mst[g] for g in range(n_g)])
    num = 0.0
    den = 0.0
    for g in range(n_g):
        wgt = jnp.exp(mst[g] - m_top)
        num = num + wgt * acc[g]
        den = den + wgt * lst[g]
    o_ref[...] = num / den


def _attn_prompt(q, k, v):
    B, T, W = q.shape
    tq = W_MAX
    assert T % tq == 0 and W % LANES == 0
    n_hp = W // LANES
    cur = pl.BlockSpec((None, tq, LANES), lambda b, h, i: (b, i, h))
    prev = pl.BlockSpec((None, tq, LANES), lambda b, h, i: (b, jnp.maximum(i - 1, 0), h))
    n_g = len(DILATED_GROUPS)
    n_blk, qb2 = tq // Q_BLOCK, 2 * Q_BLOCK
    return pl.pallas_call(
        functools.partial(_attn_prompt_kernel, tq),
        grid=(B, n_hp, T // tq),
        in_specs=[cur, prev, cur, prev, cur],
        out_specs=cur,
        out_shape=jax.ShapeDtypeStruct((B, T, W), F32),
        scratch_shapes=[pltpu.VMEM((2 * tq, LANES), F32), pltpu.VMEM((2 * tq, LANES), F32),
                        pltpu.VMEM((n_g, tq, LANES), F32), pltpu.VMEM((n_g, tq, LANES), F32),
                        pltpu.VMEM((n_g, tq, LANES), F32), pltpu.VMEM((2, qb2, qb2), F32),
                        pltpu.VMEM((n_blk, qb2, qb2), F32), pltpu.VMEM((n_blk, qb2, qb2), BF16),
                        pltpu.VMEM((n_blk, qb2, LANES), F32), pltpu.VMEM((n_blk, qb2, LANES), F32)],
        compiler_params=_cparams(("parallel", "parallel", "arbitrary")),
        name="attn_prompt",
    )(q, k, k, v, v)


def _attn_sample_kernel(n_new, q_ref, kc_ref, vc_ref, kn_ref, vn_ref, o_ref):
    P = kc_ref.shape[0]
    W = kc_ref.shape[1]
    pad = q_ref.shape[0]
    n_rows = ATTN_HEADS * pad
    scale = HEAD_DIM ** -0.5
    q = q_ref[...]
    col_head = lax.broadcasted_iota(jnp.int32, (pad, W), 1) // HEAD_DIM
    qbd = jnp.concatenate([jnp.where(col_head == h, q, 0.0) for h in range(ATTN_HEADS)], axis=0).astype(BF16)

    def multiplicity(dist):
        cnt = jnp.zeros(dist.shape, F32)
        for w, d in DILATED_GROUPS:
            cnt = cnt + ((dist >= 0) & (dist <= w) & (dist % d == 0)).astype(F32)
        return cnt

    nt = (((1,), (1,)), ((), ()))
    s_c = lax.dot_general(qbd, kc_ref[...].astype(BF16), nt, preferred_element_type=F32) * scale
    s_n = lax.dot_general(qbd, kn_ref[...].astype(BF16), nt, preferred_element_type=F32) * scale
    j_c = lax.broadcasted_iota(jnp.int32, (n_rows, P), 0) % pad
    i_c = lax.broadcasted_iota(jnp.int32, (n_rows, P), 1)
    mult_c = multiplicity(j_c + P - i_c)
    j_n = lax.broadcasted_iota(jnp.int32, (n_rows, pad), 0) % pad
    i_n = lax.broadcasted_iota(jnp.int32, (n_rows, pad), 1)
    mult_n = jnp.where(i_n < n_new, multiplicity(j_n - i_n), 0.0)
    s_c = jnp.where(mult_c > 0, s_c, NEG_INF)
    s_n = jnp.where(mult_n > 0, s_n, NEG_INF)
    m = jnp.maximum(jnp.max(s_c, axis=-1, keepdims=True), jnp.max(s_n, axis=-1, keepdims=True))
    p_c = mult_c * jnp.exp(s_c - m)
    p_n = mult_n * jnp.exp(s_n - m)
    l = jnp.sum(p_c, axis=-1, keepdims=True) + jnp.sum(p_n, axis=-1, keepdims=True)
    pv = (_dot(p_c.astype(BF16), vc_ref[...].astype(BF16)) + _dot(p_n.astype(BF16), vn_ref[...].astype(BF16))) / l
    out = jnp.zeros((pad, W), F32)
    for h in range(ATTN_HEADS):
        out = out + jnp.where(col_head == h, pv[h * pad:(h + 1) * pad], 0.0)
    o_ref[...] = out


def _attn_sample(q, k_new, v_new, k_cache, v_cache, n_new):
    B, pad, W = q.shape
    P = k_cache.shape[1]
    small = pl.BlockSpec((None, pad, W), lambda b: (b, 0, 0))
    big = pl.BlockSpec((None, P, W), lambda b: (b, 0, 0))
    return pl.pallas_call(
        functools.partial(_attn_sample_kernel, n_new),
        grid=(B,),
        in_specs=[small, big, big, small, small],
        out_specs=small,
        out_shape=jax.ShapeDtypeStruct((B, pad, W), F32),
        compiler_params=_cparams(("parallel",)),
        name="attn_sample",
    )(q, k_cache, v_cache, k_new, v_new)


def _merge_kernel(x_ref, gpre_ref, wg_ref, ya_ref, yb_ref, yc_ref, wa_ref, wb_ref, wc_ref, wo_ref, gpost_ref, o_ref):
    x = x_ref[...]
    D = x.shape[1]
    h = _rms(x, gpre_ref[...]).astype(BF16)
    merged = None
    for i, (y_ref, w_ref) in enumerate(((ya_ref, wa_ref), (yb_ref, wb_ref), (yc_ref, wc_ref))):
        gate = jax.nn.sigmoid(_dot(h, wg_ref[:, i * D:(i + 1) * D]))
        term = gate * _dot(y_ref[...].astype(BF16), w_ref[...])
        merged = term if merged is None else merged + term
    t = _dot(merged.astype(BF16), wo_ref[...])
    o_ref[...] = x + _rms(t, gpost_ref[...])


def _merge(x, ya, yb, yc, p, rows):
    B, T, D = x.shape
    spec = lambda a: pl.BlockSpec((None, rows, a.shape[2]), lambda b, i: (b, i, 0))
    xspec = spec(x)
    consts = [p['w_proj_a'], p['w_proj_b'], p['w_proj_c'], p['w_out'], p['norm_post_mix']]
    return pl.pallas_call(
        _merge_kernel,
        grid=(B, T // rows),
        in_specs=[xspec, _const_spec((1, D)), _const_spec(p['w_gates'].shape), spec(ya), spec(yb), spec(yc)]
                 + [_const_spec(a.shape) for a in consts],
        out_specs=xspec,
        out_shape=jax.ShapeDtypeStruct((B, T, D), F32),
        compiler_params=_cparams(("parallel", "parallel")),
        name="merge",
    )(x, p['norm_pre_mix'], p['w_gates'], ya, yb, yc, *consts)


def _mlp_kernel(ff_chunk, x_ref, gpre_ref, w1_ref, w2_ref, gpost_ref, o_ref):
    x = x_ref[...]
    h = _rms(x, gpre_ref[...]).astype(BF16)
    acc = None
    for c0 in range(0, w1_ref.shape[1], ff_chunk):
        hid = jnp.square(jnp.maximum(_dot(h, w1_ref[:, c0:c0 + ff_chunk]), 0.0)).astype(BF16)
        part = _dot(hid, w2_ref[c0:c0 + ff_chunk, :])
        acc = part if acc is None else acc + part
    o_ref[...] = x + _rms(acc, gpost_ref[...])


def _mlp(x, p, rows):
    B, T, D = x.shape
    xspec = pl.BlockSpec((None, rows, D), lambda b, i: (b, i, 0))
    return pl.pallas_call(
        functools.partial(_mlp_kernel, 1024),
        grid=(B, T // rows),
        in_specs=[xspec, _const_spec((1, D)), _const_spec(p['mlp_w1'].shape), _const_spec(p['mlp_w2'].shape),
                  _const_spec((1, D))],
        out_specs=xspec,
        out_shape=jax.ShapeDtypeStruct((B, T, D), F32),
        compiler_params=_cparams(("parallel", "parallel")),
        name="mlp",
    )(x, p['norm_pre_mlp'], p['mlp_w1'], p['mlp_w2'], p['norm_post_mlp'])


def _block_diag_tiles(w, per_tile):
    n, c, d = w.shape
    w = w.reshape(n // per_tile, per_tile, c, d)
    eye = jnp.eye(per_tile, dtype=w.dtype)
    return jnp.einsum('jpcd,pq->jpcqd', w, eye).reshape(n // per_tile, per_tile * c, per_tile * d)


def _layer_params(l, a):
    D = a['w_in'].shape[1]
    lru_w = a['conv_w'].shape[2]
    n_blk, blk = a['lru_wr'].shape[1], a['lru_wr'].shape[2]
    G, N, GC = a['ssm_b_re'].shape[1:]
    ssm_w = G * GC
    attn_w = ATTN_HEADS * HEAD_DIM
    n_front = lru_w + ssm_w + 3 * attn_w
    row = lambda v: v.reshape(1, -1).astype(F32)
    p = {
        'norm_pre_mix': row(a['norm_pre_mix'][l]), 'norm_post_mix': row(a['norm_post_mix'][l]),
        'norm_pre_mlp': row(a['norm_pre_mlp'][l]), 'norm_post_mlp': row(a['norm_post_mlp'][l]),
        'w_front': a['w_in'][l][:, :n_front].astype(BF16), 'w_gates': a['w_in'][l][:, n_front:].astype(BF16),
        'w_proj_a': a['w_proj_a'][l].astype(BF16), 'w_proj_b': a['w_proj_b'][l].astype(BF16),
        'w_proj_c': a['w_proj_c'][l].astype(BF16), 'w_out': a['w_out'][l].astype(BF16),
        'mlp_w1': a['mlp_w1'][l].astype(BF16), 'mlp_w2': a['mlp_w2'][l].astype(BF16),
        'widths': (lru_w, ssm_w, attn_w, attn_w, attn_w),
    }
    per_tile = MXU_TILE // blk
    p['lru'] = {
        'conv_w': a['conv_w'][l].astype(F32), 'conv_b': row(a['conv_b'][l]),
        'wr': _block_diag_tiles(a['lru_wr'][l], per_tile).astype(BF16), 'br': row(a['lru_br'][l]),
        'wi': _block_diag_tiles(a['lru_wi'][l], per_tile).astype(BF16), 'bi': row(a['lru_bi'][l]),
        'lam': row(a['lru_lambda'][l]),
    }
    lr, li = a['ssm_lambda_re'][l].astype(F32), a['ssm_lambda_im'][l].astype(F32)
    dt = jnp.exp(a['ssm_log_dt'][l].astype(F32))[:, None]
    mag = jnp.exp(lr * dt)
    ang = li * dt
    ab_re, ab_im = mag * jnp.cos(ang), mag * jnp.sin(ang)
    den = lr * lr + li * li
    f_re = ((ab_re - 1.0) * lr + ab_im * li) / den
    f_im = (ab_im * lr - (ab_re - 1.0) * li) / den
    b_re, b_im = a['ssm_b_re'][l].astype(F32), a['ssm_b_im'][l].astype(F32)
    bb_re = f_re[..., None] * b_re - f_im[..., None] * b_im
    bb_im = f_re[..., None] * b_im + f_im[..., None] * b_re
    g_half = G // 2
    to_b = lambda m: _block_diag_tiles(jnp.swapaxes(m, 1, 2), g_half).astype(BF16)
    to_c = lambda m: _block_diag_tiles(jnp.swapaxes(m, 1, 2), g_half).astype(BF16)
    p['s5'] = {
        'a_re': ab_re.reshape(1, G * N), 'a_im': ab_im.reshape(1, G * N),
        'b_re': to_b(bb_re), 'b_im': to_b(bb_im),
        'c_re': to_c(a['ssm_c_re'][l].astype(F32)), 'c_im': to_c(a['ssm_c_im'][l].astype(F32)),
        'd': row(a['ssm_d'][l]), 'glu_w': a['glu_w'][l].astype(BF16), 'glu_b': row(a['glu_b'][l]),
    }
    return p


def _tm(state):
    nb, k, w = state.shape
    return jnp.swapaxes(state, 0, 1).reshape(k * nb, w)


def _layer_prompt(x, p, rows, tt):
    B, T, D = x.shape
    lru_w, ssm_w, attn_w = p['widths'][0], p['widths'][1], p['widths'][2]
    xa, u, q, k, v = _in_proj(x, p['norm_pre_mix'], p['w_front'], p['widths'], rows)
    n_state = p['s5']['a_re'].shape[1]
    halo = (p['lru']['conv_w'].shape[0] - 1) * B
    ya, nconv, hlast = _rglru(xa, jnp.zeros((halo, lru_w), F32), jnp.zeros((B, lru_w), F32), p['lru'], B, tt)
    yb, s_re, s_im = _s5(u, jnp.zeros((B, n_state), F32), jnp.zeros((B, n_state), F32), p['s5'], B, tt)
    yc = _attn_prompt(q, k, v)
    x = _merge(x, ya, yb, yc, p, rows)
    x = _mlp(x, p, rows)
    return x, nconv, hlast, s_re, s_im, k, v


def _layer_sample(x, conv_state, h0, s0r, s0i, k_cache, v_cache, p, nb, n_new):
    rows = x.shape[1]
    lru_w, ssm_w, attn_w = p['widths'][0], p['widths'][1], p['widths'][2]
    xa, u, q, k, v = _in_proj(x, p['norm_pre_mix'], p['w_front'], p['widths'], rows)
    ya, nconv, hlast = _rglru(xa[0], conv_state, h0, p['lru'], nb, n_new)
    yb, s_re, s_im = _s5(u[0], s0r, s0i, p['s5'], nb, n_new)
    pad = 8

    def bm(arr):
        arr = jnp.swapaxes(arr.reshape(n_new, nb, attn_w), 0, 1)
        return jnp.pad(arr, ((0, 0), (0, pad - n_new), (0, 0)))

    yc = _attn_sample(bm(q), bm(k), bm(v), k_cache, v_cache, n_new)[:, :n_new]
    yc = jnp.swapaxes(yc, 0, 1).reshape(1, rows, attn_w)
    x = _merge(x, ya[None], yb[None], yc, p, rows)
    x = _mlp(x, p, rows)
    return x, nconv, hlast, s_re, s_im, k, v


def kernel(x_prompt, x_sample, state_conv, state_lru, state_ssm_re, state_ssm_im, cache_k, cache_v, norm_pre_mix, norm_post_mix, norm_pre_mlp, norm_post_mlp, w_in, conv_w, conv_b, lru_wr, lru_br, lru_wi, lru_bi, lru_lambda, ssm_lambda_re, ssm_lambda_im, ssm_log_dt, ssm_b_re, ssm_b_im, ssm_c_re, ssm_c_im, ssm_d, glu_w, glu_b, w_proj_a, w_proj_b, w_proj_c, w_out, mlp_w1, mlp_w2):
    a = dict(norm_pre_mix=norm_pre_mix, norm_post_mix=norm_post_mix, norm_pre_mlp=norm_pre_mlp,
             norm_post_mlp=norm_post_mlp, w_in=w_in, conv_w=conv_w, conv_b=conv_b, lru_wr=lru_wr, lru_br=lru_br,
             lru_wi=lru_wi, lru_bi=lru_bi, lru_lambda=lru_lambda, ssm_lambda_re=ssm_lambda_re,
             ssm_lambda_im=ssm_lambda_im, ssm_log_dt=ssm_log_dt, ssm_b_re=ssm_b_re, ssm_b_im=ssm_b_im,
             ssm_c_re=ssm_c_re, ssm_c_im=ssm_c_im, ssm_d=ssm_d, glu_w=glu_w, glu_b=glu_b, w_proj_a=w_proj_a,
             w_proj_b=w_proj_b, w_proj_c=w_proj_c, w_out=w_out, mlp_w1=mlp_w1, mlp_w2=mlp_w2)
    depth = w_in.shape[0]
    Bp, Tp, D = x_prompt.shape
    Bs, Ts, _ = x_sample.shape
    G, N = state_ssm_re.shape[2:]
    attn_w = ATTN_HEADS * HEAD_DIM
    P = cache_k.shape[2]
    assert P == W_MAX, "the window buffer must hold exactly the widest window"
    keep = min(W_MAX, Tp)
    conv_k = state_conv.shape[2]

    xp = x_prompt
    xs = jnp.swapaxes(x_sample, 0, 1).reshape(1, Ts * Bs, D)
    outs = [[] for _ in range(12)]
    for l in range(depth):
        p = _layer_params(l, a)
        xp, c_, h_, r_, i_, k_, v_ = _layer_prompt(xp, p, rows=512, tt=64)
        pk = k_[:, Tp - keep:].reshape(Bp, keep, ATTN_HEADS, HEAD_DIM)
        pv = v_[:, Tp - keep:].reshape(Bp, keep, ATTN_HEADS, HEAD_DIM)
        for lst, val in zip(outs[:6], (jnp.swapaxes(c_.reshape(conv_k, Bp, -1), 0, 1), h_,
                                       r_.reshape(Bp, G, N), i_.reshape(Bp, G, N), pk, pv)):
            lst.append(val)
        xs, c_, h_, r_, i_, k_, v_ = _layer_sample(
            xs, _tm(state_conv[l]), state_lru[l], state_ssm_re[l].reshape(Bs, G * N),
            state_ssm_im[l].reshape(Bs, G * N), cache_k[l].reshape(Bs, P, attn_w),
            cache_v[l].reshape(Bs, P, attn_w), p, Bs, Ts)
        sk = jnp.swapaxes(k_.reshape(Ts, Bs, ATTN_HEADS, HEAD_DIM), 0, 1)
        sv = jnp.swapaxes(v_.reshape(Ts, Bs, ATTN_HEADS, HEAD_DIM), 0, 1)
        for lst, val in zip(outs[6:], (jnp.swapaxes(c_.reshape(conv_k, Bs, -1), 0, 1), h_,
                                       r_.reshape(Bs, G, N), i_.reshape(Bs, G, N), sk, sv)):
            lst.append(val)
    ys = jnp.swapaxes(xs.reshape(Ts, Bs, D), 0, 1)
    return (xp, ys) + tuple(jnp.stack(o) for o in outs)
```

```python
import functools
import math

import jax
import jax.numpy as jnp
from jax import lax
from jax.experimental import pallas as pl
from jax.experimental.pallas import tpu as pltpu

F32 = jnp.float32
BF16 = jnp.bfloat16

RMS_EPS = 1e-6
NEG_INF = -1e30
RG_C = 8.0
ATTN_HEADS = 12
HEAD_DIM = 64
DILATED_GROUPS = ((128, 1), (512, 4), (2048, 16))
W_MAX = max(w for w, _ in DILATED_GROUPS)
Q_BLOCK = 128

LANES = 128
MXU_TILE = 256
VMEM_LIMIT = 56 * 1024 * 1024


def _cparams(sem):
    return pltpu.CompilerParams(dimension_semantics=sem, vmem_limit_bytes=VMEM_LIMIT)


def _rms(x, g):
    return x * lax.rsqrt(jnp.mean(x * x, axis=-1, keepdims=True) + RMS_EPS) * g


def _dot(a, b):
    return jnp.dot(a, b, preferred_element_type=F32)


def _expm1(x):
    e = jnp.exp(x)
    return jnp.where(jnp.abs(x) > 0.5, e - 1.0, jnp.tanh(0.5 * x) * (e + 1.0))


def _const_spec(shape):
    nd = len(shape)
    return pl.BlockSpec(shape, lambda *_: (0,) * nd)


def _in_proj_kernel(splits, tail_from, x_ref, g_ref, w_ref, *out_refs):
    n = len(splits)
    h = _rms(x_ref[...], g_ref[...]).astype(BF16)
    for (lo, hi), o_ref in zip(splits, out_refs[:n]):
        o_ref[...] = _dot(h, w_ref[:, lo:hi])
    if len(out_refs) > n:
        @pl.when(pl.program_id(1) >= tail_from)
        def _():
            for src, dst in zip(out_refs[n - 2:n], out_refs[n:]):
                dst[...] = src[...].reshape(dst.shape)


def _in_proj(x, g, w, widths, rows, keep=0):
    B, T, D = x.shape
    splits, lo = [], 0
    for wd in widths:
        splits.append((lo, lo + wd))
        lo += wd
    out_specs = [pl.BlockSpec((None, rows, wd), lambda b, i: (b, i, 0)) for wd in widths]
    out_shape = [jax.ShapeDtypeStruct((B, T, wd), F32) for wd in widths]
    tail_from = (T - keep) // rows
    if keep:
        assert keep % rows == 0 and widths[-1] == widths[-2] == ATTN_HEADS * HEAD_DIM
        tail = pl.BlockSpec((None, rows, ATTN_HEADS, HEAD_DIM), lambda b, i: (b, jnp.maximum(i - tail_from, 0), 0, 0))
        out_specs += [tail, tail]
        out_shape += [jax.ShapeDtypeStruct((B, keep, ATTN_HEADS, HEAD_DIM), F32)] * 2
    return pl.pallas_call(
        functools.partial(_in_proj_kernel, tuple(splits), tail_from),
        grid=(B, T // rows),
        in_specs=[pl.BlockSpec((None, rows, D), lambda b, i: (b, i, 0)),
                  _const_spec((1, D)), _const_spec(w.shape)],
        out_specs=out_specs,
        out_shape=out_shape,
        compiler_params=_cparams(("parallel", "arbitrary")),
        name="in_proj",
    )(x, g.reshape(1, D), w)


def _load_time_major(ref):
    if len(ref.shape) == 2:
        return ref[...]
    nb, tt, w = ref.shape
    return jnp.swapaxes(ref[...], 0, 1).reshape(tt * nb, w)


def _store_time_major(ref, val):
    if len(ref.shape) == 2:
        ref[...] = val.astype(ref.dtype)
    else:
        nb, tt, w = ref.shape
        ref[...] = jnp.swapaxes(val.reshape(tt, nb, w), 0, 1).astype(ref.dtype)


def _seq_specs(x, nb, tt):
    w = x.shape[-1]
    if x.ndim == 3:
        return pl.BlockSpec((nb, tt, w), lambda i: (0, i, 0)), x.shape[1]
    return pl.BlockSpec((tt * nb, w), lambda i: (i, 0)), x.shape[0] // nb


def _rglru_kernel(nb, tt, xa_ref, cs_ref, h0_ref, cw_ref, cb_ref, wr_ref, br_ref, wi_ref, bi_ref, lam_ref,
                  ya_ref, nconv_ref, hlast_ref, xe_buf, a_buf, b_buf, h_carry):
    rows = tt * nb
    halo = cs_ref.shape[0]
    n_tap = cw_ref.shape[0]

    @pl.when(pl.program_id(0) == 0)
    def _():
        xe_buf[0:halo] = cs_ref[...]
        h_carry[...] = h0_ref[...]

    xe_buf[halo:halo + rows] = _load_time_major(xa_ref)
    xc = cb_ref[...] + sum(cw_ref[i:i + 1, :] * xe_buf[i * nb:i * nb + rows] for i in range(n_tap))
    tail = xe_buf[rows:rows + halo]
    nconv_ref[...] = tail
    xe_buf[0:halo] = tail

    xcb = xc.astype(BF16)
    c = -RG_C * jax.nn.softplus(-lam_ref[...])
    for j in range(wr_ref.shape[0]):
        sl = slice(j * MXU_TILE, (j + 1) * MXU_TILE)
        r = jax.nn.sigmoid(_dot(xcb[:, sl], wr_ref[j]) + br_ref[:, sl])
        ig = jax.nn.sigmoid(_dot(xcb[:, sl], wi_ref[j]) + bi_ref[:, sl])
        log_a = c[:, sl] * r
        a_buf[:, sl] = jnp.exp(log_a)
        mult = jnp.sqrt(jnp.maximum(-_expm1(2.0 * log_a), 0.0))
        b_buf[:, sl] = mult * (ig * xc[:, sl])

    def step(t, h):
        rs = pl.ds(pl.multiple_of(t * nb, nb), nb)
        h = a_buf[rs, :] * h + b_buf[rs, :]
        a_buf[rs, :] = h
        return h

    h = lax.fori_loop(0, tt, step, h_carry[...])
    h_carry[...] = h
    hlast_ref[...] = h
    _store_time_major(ya_ref, a_buf[...])


def _rglru(xa, conv_state, h0, p, nb, tt):
    W = xa.shape[-1]
    seq_spec, T = _seq_specs(xa, nb, tt)
    rows = tt * nb
    halo = conv_state.shape[0]
    assert rows >= halo and T % tt == 0
    consts = [p['conv_w'], p['conv_b'], p['wr'], p['br'], p['wi'], p['bi'], p['lam']]
    return pl.pallas_call(
        functools.partial(_rglru_kernel, nb, tt),
        grid=(T // tt,),
        in_specs=[seq_spec, _const_spec(conv_state.shape), _const_spec(h0.shape)]
                 + [_const_spec(a.shape) for a in consts],
        out_specs=[seq_spec, _const_spec(conv_state.shape), _const_spec(h0.shape)],
        out_shape=[jax.ShapeDtypeStruct(xa.shape, BF16), jax.ShapeDtypeStruct(conv_state.shape, F32),
                   jax.ShapeDtypeStruct(h0.shape, F32)],
        scratch_shapes=[pltpu.VMEM((rows + halo, W), F32), pltpu.VMEM((rows, W), F32),
                        pltpu.VMEM((rows, W), F32), pltpu.VMEM((nb, W), F32)],
        compiler_params=_cparams(("arbitrary",)),
        name="rglru",
    )(xa, conv_state, h0, *consts)


def _gelu_tanh(x):
    return 0.5 * x * (1.0 + jnp.tanh(math.sqrt(2.0 / math.pi) * (x + 0.044715 * (x * x * x))))


def _s5_kernel(nb, tt, chunk, u_ref, s0r_ref, s0i_ref, ar_ref, ai_ref, bre_ref, bim_ref, cre_ref, cim_ref,
               d_ref, gw_ref, gb_ref, yb_ref, lastr_ref, lasti_ref, sre, sim, car_r, car_i):
    n_state = sre.shape[1]
    n_half = bre_ref.shape[0]
    kw = bre_ref.shape[1]
    sw = bre_ref.shape[2]
    yw = cre_ref.shape[2]

    @pl.when(pl.program_id(0) == 0)
    def _():
        car_r[...] = s0r_ref[...]
        car_i[...] = s0i_ref[...]

    u = _load_time_major(u_ref)
    ub = u.astype(BF16)
    for hf in range(n_half):
        sre[:, hf * sw:(hf + 1) * sw] = _dot(ub[:, hf * kw:(hf + 1) * kw], bre_ref[hf])
        sim[:, hf * sw:(hf + 1) * sw] = _dot(ub[:, hf * kw:(hf + 1) * kw], bim_ref[hf])

    for c0 in range(0, n_state, chunk):
        cs = slice(c0, c0 + chunk)
        ar = jnp.broadcast_to(ar_ref[:, cs], (nb, chunk))
        ai = jnp.broadcast_to(ai_ref[:, cs], (nb, chunk))

        def step(t, carry):
            s_r, s_i = carry
            rs = pl.ds(pl.multiple_of(t * nb, nb), nb)
            n_r = ar * s_r - ai * s_i + sre[rs, cs]
            n_i = ar * s_i + ai * s_r + sim[rs, cs]
            sre[rs, cs] = n_r
            sim[rs, cs] = n_i
            return n_r, n_i

        s_r, s_i = lax.fori_loop(0, tt, step, (car_r[:, cs], car_i[:, cs]))
        car_r[:, cs] = s_r
        car_i[:, cs] = s_i

    lastr_ref[...] = car_r[...]
    lasti_ref[...] = car_i[...]

    ys = []
    for hf in range(n_half):
        ss = slice(hf * sw, (hf + 1) * sw)
        ys.append(_dot(sre[:, ss].astype(BF16), cre_ref[hf]) - _dot(sim[:, ss].astype(BF16), cim_ref[hf]))
    y = jnp.concatenate(ys, axis=1) + d_ref[...] * u
    z = _gelu_tanh(y).astype(BF16)
    w_half = gw_ref.shape[1] // 2
    val = _dot(z, gw_ref[:, :w_half]) + gb_ref[:, :w_half]
    gate = _dot(z, gw_ref[:, w_half:]) + gb_ref[:, w_half:]
    _store_time_major(yb_ref, val * jax.nn.sigmoid(gate))


def _s5(u, s0r, s0i, p, nb, tt):
    seq_spec, T = _seq_specs(u, nb, tt)
    rows = tt * nb
    n_state = s0r.shape[1]
    chunk = max(LANES, min(n_state, 4 * LANES * 8 // nb))
    consts = [p['a_re'], p['a_im'], p['b_re'], p['b_im'], p['c_re'], p['c_im'], p['d'], p['glu_w'], p['glu_b']]
    return pl.pallas_call(
        functools.partial(_s5_kernel, nb, tt, chunk),
        grid=(T // tt,),
        in_specs=[seq_spec, _const_spec(s0r.shape), _const_spec(s0i.shape)]
                 + [_const_spec(a.shape) for a in consts],
        out_specs=[seq_spec, _const_spec(s0r.shape), _const_spec(s0i.shape)],
        out_shape=[jax.ShapeDtypeStruct(u.shape, BF16), jax.ShapeDtypeStruct(s0r.shape, F32),
                   jax.ShapeDtypeStruct(s0i.shape, F32)],
        scratch_shapes=[pltpu.VMEM((rows, n_state), F32), pltpu.VMEM((rows, n_state), F32),
                        pltpu.VMEM((nb, n_state), F32), pltpu.VMEM((nb, n_state), F32)],
        compiler_params=_cparams(("arbitrary",)),
        name="s5",
    )(u, s0r, s0i, *consts)


def _attn_prompt_kernel(tq, q_ref, kp_ref, kc_ref, vp_ref, vc_ref, o_ref,
                        kall, vall, acc, mst, lst, mask_t, one_hot, sbuf, pbuf):
    it = pl.program_id(2)
    kall[0:tq] = kp_ref[...]
    kall[tq:2 * tq] = kc_ref[...]
    vall[0:tq] = vp_ref[...]
    vall[tq:2 * tq] = vc_ref[...]

    qb = Q_BLOCK
    scale = HEAD_DIM ** -0.5
    lane = lax.broadcasted_iota(jnp.int32, (1, LANES), 1)
    head0 = lane < HEAD_DIM
    key = lax.broadcasted_iota(jnp.int32, (2 * qb, qb), 0)
    row = lax.broadcasted_iota(jnp.int32, (2 * qb, qb), 1)
    band_t = (key >= row) & (key <= row + qb)
    mask_t[0] = jnp.where(band_t, 0.0, NEG_INF).astype(BF16)
    mask_t[1] = jnp.where(band_t & (key >= qb), 0.0, NEG_INF).astype(BF16)
    one_hot[...] = (key % qb == row).astype(BF16)

    blocks = []
    for g, (w, d) in enumerate(DILATED_GROUPS):
        assert w // d == qb
        nblk = tq // (qb * d)
        for idx in range(d * nblk):
            qstart = idx // nblk + d * qb * (idx % nblk)
            kstart = tq + qstart - d * qb
            stride = {} if d == 1 else {"stride": d}
            blocks.append((g, qstart - d * qb < 0, pl.ds(qstart, qb, **stride), pl.ds(kstart, 2 * qb, **stride)))
    n_slot = sbuf.shape[0]
    first_tile = (it == 0).astype(jnp.int32)

    def scores(j):
        _, before_tile, qs, ks = blocks[j]
        qt = q_ref[qs, :] * scale
        q2 = jnp.concatenate([jnp.where(head0, qt, 0.0), jnp.where(head0, 0.0, qt)], axis=0).astype(BF16)
        q_aug = jnp.concatenate([q2, one_hot[...]], axis=1)
        k_aug = jnp.concatenate([kall[ks, :].astype(BF16), mask_t[first_tile] if before_tile else mask_t[0]], axis=1)
        sbuf[j % n_slot] = lax.dot_general(q_aug, k_aug, (((1,), (1,)), ((), ())), preferred_element_type=F32)

    def probs(j):
        g, _, qs, _ = blocks[j]
        sl = j % n_slot
        m = jnp.max(jnp.maximum(sbuf[sl, :, 0:LANES], sbuf[sl, :, LANES:2 * LANES]), axis=-1, keepdims=True)
        pbuf[sl, :, 0:LANES] = jnp.exp(sbuf[sl, :, 0:LANES] - m).astype(BF16)
        pbuf[sl, :, LANES:2 * LANES] = jnp.exp(sbuf[sl, :, LANES:2 * LANES] - m).astype(BF16)
        mst[g, qs, :] = jnp.where(head0, m[0:qb], m[qb:2 * qb])

    def values(j):
        g, _, qs, ks = blocks[j]
        v_aug = jnp.concatenate([vall[ks, :].astype(BF16), jnp.ones((2 * qb, LANES), BF16)], axis=1)
        pv = _dot(pbuf[j % n_slot], v_aug)
        acc[g, qs, :] = jnp.where(head0, pv[0:qb, 0:LANES], pv[qb:2 * qb, 0:LANES])
        lst[g, qs, :] = jnp.where(head0, pv[0:qb, LANES:2 * LANES], pv[qb:2 * qb, LANES:2 * LANES])

    group = 2
    n_chunk = len(blocks) // group
    for step in range(n_chunk + 2):
        @pl.when(it >= 0)
        def _(step=step):
            for c, stage in ((step - 2, values), (step - 1, probs), (step, scores)):
                if 0 <= c < n_chunk:
                    for j in range(c * group, (c + 1) * group):
                        stage(j)

    n_g = len(DILATED_GROUPS)
    m_top = functools.reduce(jnp.maximum, [mst[g] for g in range(n_g)])
    num = 0.0
    den = 0.0
    for g in range(n_g):
        wgt = jnp.exp(mst[g] - m_top)
        num = num + wgt * acc[g]
        den = den + wgt * lst[g]
    o_ref[...] = (num / den).astype(o_ref.dtype)


def _attn_prompt(q, k, v):
    B, T, W = q.shape
    tq = W_MAX
    assert T % tq == 0 and W % LANES == 0
    n_hp = W // LANES
    cur = pl.BlockSpec((None, tq, LANES), lambda b, h, i: (b, i, h))
    prev = pl.BlockSpec((None, tq, LANES), lambda b, h, i: (b, jnp.maximum(i - 1, 0), h))
    n_g = len(DILATED_GROUPS)
    n_blk, qb2 = 12, 2 * Q_BLOCK
    return pl.pallas_call(
        functools.partial(_attn_prompt_kernel, tq),
        grid=(B, n_hp, T // tq),
        in_specs=[cur, prev, cur, prev, cur],
        out_specs=cur,
        out_shape=jax.ShapeDtypeStruct((B, T, W), BF16),
        scratch_shapes=[pltpu.VMEM((2 * tq, LANES), F32), pltpu.VMEM((2 * tq, LANES), F32),
                        pltpu.VMEM((n_g, tq, LANES), F32), pltpu.VMEM((n_g, tq, LANES), F32),
                        pltpu.VMEM((n_g, tq, LANES), F32),
                        pltpu.VMEM((2, qb2, Q_BLOCK), BF16), pltpu.VMEM((qb2, Q_BLOCK), BF16),
                        pltpu.VMEM((n_blk, qb2, qb2), F32), pltpu.VMEM((n_blk, qb2, qb2), BF16)],
        compiler_params=_cparams(("parallel", "parallel", "arbitrary")),
        name="attn_prompt",
    )(q, k, k, v, v)


def _attn_sample_kernel(n_new, q_ref, kc_ref, vc_ref, kn_ref, vn_ref, o_ref):
    P = kc_ref.shape[0]
    W = kc_ref.shape[1]
    pad = q_ref.shape[0]
    n_rows = ATTN_HEADS * pad
    scale = HEAD_DIM ** -0.5
    q = q_ref[...]
    col_head = lax.broadcasted_iota(jnp.int32, (pad, W), 1) // HEAD_DIM
    qbd = jnp.concatenate([jnp.where(col_head == h, q, 0.0) for h in range(ATTN_HEADS)], axis=0).astype(BF16)

    def multiplicity(dist):
        cnt = jnp.zeros(dist.shape, F32)
        for w, d in DILATED_GROUPS:
            cnt = cnt + ((dist >= 0) & (dist <= w) & (dist % d == 0)).astype(F32)
        return cnt

    nt = (((1,), (1,)), ((), ()))
    s_c = lax.dot_general(qbd, kc_ref[...].astype(BF16), nt, preferred_element_type=F32) * scale
    s_n = lax.dot_general(qbd, kn_ref[...].astype(BF16), nt, preferred_element_type=F32) * scale
    j_c = lax.broadcasted_iota(jnp.int32, (n_rows, P), 0) % pad
    i_c = lax.broadcasted_iota(jnp.int32, (n_rows, P), 1)
    mult_c = multiplicity(j_c + P - i_c)
    j_n = lax.broadcasted_iota(jnp.int32, (n_rows, pad), 0) % pad
    i_n = lax.broadcasted_iota(jnp.int32, (n_rows, pad), 1)
    mult_n = jnp.where(i_n < n_new, multiplicity(j_n - i_n), 0.0)
    s_c = jnp.where(mult_c > 0, s_c, NEG_INF)
    s_n = jnp.where(mult_n > 0, s_n, NEG_INF)
    m = jnp.maximum(jnp.max(s_c, axis=-1, keepdims=True), jnp.max(s_n, axis=-1, keepdims=True))
    p_c = mult_c * jnp.exp(s_c - m)
    p_n = mult_n * jnp.exp(s_n - m)
    l = jnp.sum(p_c, axis=-1, keepdims=True) + jnp.sum(p_n, axis=-1, keepdims=True)
    pv = (_dot(p_c.astype(BF16), vc_ref[...].astype(BF16)) + _dot(p_n.astype(BF16), vn_ref[...].astype(BF16))) / l
    out = jnp.zeros((pad, W), F32)
    for h in range(ATTN_HEADS):
        out = out + jnp.where(col_head == h, pv[h * pad:(h + 1) * pad], 0.0)
    o_ref[...] = out


def _attn_sample(q, k_new, v_new, k_cache, v_cache, n_new):
    B, pad, W = q.shape
    P = k_cache.shape[1]
    small = pl.BlockSpec((None, pad, W), lambda b: (b, 0, 0))
    big = pl.BlockSpec((None, P, W), lambda b: (b, 0, 0))
    return pl.pallas_call(
        functools.partial(_attn_sample_kernel, n_new),
        grid=(B,),
        in_specs=[small, big, big, small, small],
        out_specs=small,
        out_shape=jax.ShapeDtypeStruct((B, pad, W), F32),
        compiler_params=_cparams(("parallel",)),
        name="attn_sample",
    )(q, k_cache, v_cache, k_new, v_new)


def _merge_kernel(x_ref, gpre_ref, wg_ref, ya_ref, yb_ref, yc_ref, wa_ref, wb_ref, wc_ref, wo_ref, gpost_ref, o_ref):
    x = x_ref[...]
    D = x.shape[1]
    h = _rms(x, gpre_ref[...]).astype(BF16)
    merged = None
    for i, (y_ref, w_ref) in enumerate(((ya_ref, wa_ref), (yb_ref, wb_ref), (yc_ref, wc_ref))):
        gate = jax.nn.sigmoid(_dot(h, wg_ref[:, i * D:(i + 1) * D]))
        term = gate * _dot(y_ref[...].astype(BF16), w_ref[...])
        merged = term if merged is None else merged + term
    t = _dot(merged.astype(BF16), wo_ref[...])
    o_ref[...] = x + _rms(t, gpost_ref[...])


def _merge(x, ya, yb, yc, p, rows):
    B, T, D = x.shape
    spec = lambda a: pl.BlockSpec((None, rows, a.shape[2]), lambda b, i: (b, i, 0))
    xspec = spec(x)
    consts = [p['w_proj_a'], p['w_proj_b'], p['w_proj_c'], p['w_out'], p['norm_post_mix']]
    return pl.pallas_call(
        _merge_kernel,
        grid=(B, T // rows),
        in_specs=[xspec, _const_spec((1, D)), _const_spec(p['w_gates'].shape), spec(ya), spec(yb), spec(yc)]
                 + [_const_spec(a.shape) for a in consts],
        out_specs=xspec,
        out_shape=jax.ShapeDtypeStruct((B, T, D), F32),
        compiler_params=_cparams(("parallel", "parallel")),
        name="merge",
    )(x, p['norm_pre_mix'], p['w_gates'], ya, yb, yc, *consts)


def _mlp_kernel(ff_chunk, x_ref, gpre_ref, w1_ref, w2_ref, gpost_ref, o_ref):
    x = x_ref[...]
    h = _rms(x, gpre_ref[...]).astype(BF16)
    acc = None
    for c0 in range(0, w1_ref.shape[1], ff_chunk):
        hid = jnp.square(jnp.maximum(_dot(h, w1_ref[:, c0:c0 + ff_chunk]), 0.0)).astype(BF16)
        part = _dot(hid, w2_ref[c0:c0 + ff_chunk, :])
        acc = part if acc is None else acc + part
    o_ref[...] = x + _rms(acc, gpost_ref[...])


def _mlp(x, p, rows):
    B, T, D = x.shape
    xspec = pl.BlockSpec((None, rows, D), lambda b, i: (b, i, 0))
    return pl.pallas_call(
        functools.partial(_mlp_kernel, 1024),
        grid=(B, T // rows),
        in_specs=[xspec, _const_spec((1, D)), _const_spec(p['mlp_w1'].shape), _const_spec(p['mlp_w2'].shape),
                  _const_spec((1, D))],
        out_specs=xspec,
        out_shape=jax.ShapeDtypeStruct((B, T, D), F32),
        compiler_params=_cparams(("parallel", "parallel")),
        name="mlp",
    )(x, p['norm_pre_mlp'], p['mlp_w1'], p['mlp_w2'], p['norm_post_mlp'])


def _block_diag_tiles(w, per_tile):
    n, c, d = w.shape
    w = w.reshape(n // per_tile, per_tile, c, d)
    eye = jnp.eye(per_tile, dtype=w.dtype)
    return jnp.einsum('jpcd,pq->jpcqd', w, eye).reshape(n // per_tile, per_tile * c, per_tile * d)


def _layer_params(l, a):
    D = a['w_in'].shape[1]
    lru_w = a['conv_w'].shape[2]
    n_blk, blk = a['lru_wr'].shape[1], a['lru_wr'].shape[2]
    G, N, GC = a['ssm_b_re'].shape[1:]
    ssm_w = G * GC
    attn_w = ATTN_HEADS * HEAD_DIM
    n_front = lru_w + ssm_w + 3 * attn_w
    row = lambda v: v.reshape(1, -1).astype(F32)
    p = {
        'norm_pre_mix': row(a['norm_pre_mix'][l]), 'norm_post_mix': row(a['norm_post_mix'][l]),
        'norm_pre_mlp': row(a['norm_pre_mlp'][l]), 'norm_post_mlp': row(a['norm_post_mlp'][l]),
        'w_front': a['w_in_bf16'][l][:, :n_front], 'w_gates': a['w_in_bf16'][l][:, n_front:],
        'w_proj_a': a['w_proj_a'][l].astype(BF16), 'w_proj_b': a['w_proj_b'][l].astype(BF16),
        'w_proj_c': a['w_proj_c'][l].astype(BF16), 'w_out': a['w_out'][l].astype(BF16),
        'mlp_w1': a['mlp_w1'][l].astype(BF16), 'mlp_w2': a['mlp_w2'][l].astype(BF16),
        'widths': (lru_w, ssm_w, attn_w, attn_w, attn_w),
    }
    per_tile = MXU_TILE // blk
    p['lru'] = {
        'conv_w': a['conv_w'][l].astype(F32), 'conv_b': row(a['conv_b'][l]),
        'wr': _block_diag_tiles(a['lru_wr'][l], per_tile).astype(BF16), 'br': row(a['lru_br'][l]),
        'wi': _block_diag_tiles(a['lru_wi'][l], per_tile).astype(BF16), 'bi': row(a['lru_bi'][l]),
        'lam': row(a['lru_lambda'][l]),
    }
    lr, li = a['ssm_lambda_re'][l].astype(F32), a['ssm_lambda_im'][l].astype(F32)
    dt = jnp.exp(a['ssm_log_dt'][l].astype(F32))[:, None]
    mag = jnp.exp(lr * dt)
    ang = li * dt
    ab_re, ab_im = mag * jnp.cos(ang), mag * jnp.sin(ang)
    den = lr * lr + li * li
    f_re = ((ab_re - 1.0) * lr + ab_im * li) / den
    f_im = (ab_im * lr - (ab_re - 1.0) * li) / den
    b_re, b_im = a['ssm_b_re'][l].astype(F32), a['ssm_b_im'][l].astype(F32)
    bb_re = f_re[..., None] * b_re - f_im[..., None] * b_im
    bb_im = f_re[..., None] * b_im + f_im[..., None] * b_re
    g_half = G // 2
    to_b = lambda m: _block_diag_tiles(jnp.swapaxes(m, 1, 2), g_half).astype(BF16)
    to_c = lambda m: _block_diag_tiles(jnp.swapaxes(m, 1, 2), g_half).astype(BF16)
    p['s5'] = {
        'a_re': ab_re.reshape(1, G * N), 'a_im': ab_im.reshape(1, G * N),
        'b_re': to_b(bb_re), 'b_im': to_b(bb_im),
        'c_re': to_c(a['ssm_c_re'][l].astype(F32)), 'c_im': to_c(a['ssm_c_im'][l].astype(F32)),
        'd': row(a['ssm_d'][l]), 'glu_w': a['glu_w'][l].astype(BF16), 'glu_b': row(a['glu_b'][l]),
    }
    return p


def _tm(state):
    nb, k, w = state.shape
    return jnp.swapaxes(state, 0, 1).reshape(k * nb, w)


def _layer_prompt(x, p, rows, tt, keep):
    B, T, D = x.shape
    lru_w, ssm_w, attn_w = p['widths'][0], p['widths'][1], p['widths'][2]
    xa, u, q, k, v, k_tail, v_tail = _in_proj(x, p['norm_pre_mix'], p['w_front'], p['widths'], rows, keep)
    n_state = p['s5']['a_re'].shape[1]
    halo = (p['lru']['conv_w'].shape[0] - 1) * B
    ya, nconv, hlast = _rglru(xa, jnp.zeros((halo, lru_w), F32), jnp.zeros((B, lru_w), F32), p['lru'], B, tt)
    yb, s_re, s_im = _s5(u, jnp.zeros((B, n_state), F32), jnp.zeros((B, n_state), F32), p['s5'], B, tt)
    yc = _attn_prompt(q, k, v)
    x = _merge(x, ya, yb, yc, p, rows)
    x = _mlp(x, p, rows)
    return x, nconv, hlast, s_re, s_im, k_tail, v_tail


def _layer_sample(x, conv_state, h0, s0r, s0i, k_cache, v_cache, p, nb, n_new):
    rows = x.shape[1]
    lru_w, ssm_w, attn_w = p['widths'][0], p['widths'][1], p['widths'][2]
    xa, u, q, k, v = _in_proj(x, p['norm_pre_mix'], p['w_front'], p['widths'], rows)
    ya, nconv, hlast = _rglru(xa[0], conv_state, h0, p['lru'], nb, n_new)
    yb, s_re, s_im = _s5(u[0], s0r, s0i, p['s5'], nb, n_new)
    pad = 8

    def bm(arr):
        arr = jnp.swapaxes(arr.reshape(n_new, nb, attn_w), 0, 1)
        return jnp.pad(arr, ((0, 0), (0, pad - n_new), (0, 0)))

    yc = _attn_sample(bm(q), bm(k), bm(v), k_cache, v_cache, n_new)[:, :n_new]
    yc = jnp.swapaxes(yc, 0, 1).reshape(1, rows, attn_w)
    x = _merge(x, ya[None], yb[None], yc, p, rows)
    x = _mlp(x, p, rows)
    return x, nconv, hlast, s_re, s_im, k, v


def kernel(x_prompt, x_sample, state_conv, state_lru, state_ssm_re, state_ssm_im, cache_k, cache_v, norm_pre_mix, norm_post_mix, norm_pre_mlp, norm_post_mlp, w_in, conv_w, conv_b, lru_wr, lru_br, lru_wi, lru_bi, lru_lambda, ssm_lambda_re, ssm_lambda_im, ssm_log_dt, ssm_b_re, ssm_b_im, ssm_c_re, ssm_c_im, ssm_d, glu_w, glu_b, w_proj_a, w_proj_b, w_proj_c, w_out, mlp_w1, mlp_w2):
    a = dict(norm_pre_mix=norm_pre_mix, norm_post_mix=norm_post_mix, norm_pre_mlp=norm_pre_mlp,
             norm_post_mlp=norm_post_mlp, w_in=w_in, conv_w=conv_w, conv_b=conv_b, lru_wr=lru_wr, lru_br=lru_br,
             lru_wi=lru_wi, lru_bi=lru_bi, lru_lambda=lru_lambda, ssm_lambda_re=ssm_lambda_re,
             ssm_lambda_im=ssm_lambda_im, ssm_log_dt=ssm_log_dt, ssm_b_re=ssm_b_re, ssm_b_im=ssm_b_im,
             ssm_c_re=ssm_c_re, ssm_c_im=ssm_c_im, ssm_d=ssm_d, glu_w=glu_w, glu_b=glu_b, w_proj_a=w_proj_a,
             w_proj_b=w_proj_b, w_proj_c=w_proj_c, w_out=w_out, mlp_w1=mlp_w1, mlp_w2=mlp_w2,
             w_in_bf16=w_in.astype(BF16))
    depth = w_in.shape[0]
    Bp, Tp, D = x_prompt.shape
    Bs, Ts, _ = x_sample.shape
    G, N = state_ssm_re.shape[2:]
    attn_w = ATTN_HEADS * HEAD_DIM
    P = cache_k.shape[2]
    assert P == W_MAX, "the window buffer must hold exactly the widest window"
    keep = min(W_MAX, Tp)
    conv_k = state_conv.shape[2]

    xp = x_prompt
    xs = jnp.swapaxes(x_sample, 0, 1).reshape(1, Ts * Bs, D)
    outs = [[] for _ in range(12)]
    for l in range(depth):
        p = _layer_params(l, a)
        xp, c_, h_, r_, i_, pk, pv = _layer_prompt(xp, p, rows=512, tt=64, keep=keep)
        for lst, val in zip(outs[:6], (jnp.swapaxes(c_.reshape(conv_k, Bp, -1), 0, 1), h_,
                                       r_.reshape(Bp, G, N), i_.reshape(Bp, G, N), pk, pv)):
            lst.append(val)
        xs, c_, h_, r_, i_, k_, v_ = _layer_sample(
            xs, _tm(state_conv[l]), state_lru[l], state_ssm_re[l].reshape(Bs, G * N),
            state_ssm_im[l].reshape(Bs, G * N), cache_k[l].reshape(Bs, P, attn_w),
            cache_v[l].reshape(Bs, P, attn_w), p, Bs, Ts)
        sk = jnp.swapaxes(k_.reshape(Ts, Bs, ATTN_HEADS, HEAD_DIM), 0, 1)
        sv = jnp.swapaxes(v_.reshape(Ts, Bs, ATTN_HEADS, HEAD_DIM), 0, 1)
        for lst, val in zip(outs[6:], (jnp.swapaxes(c_.reshape(conv_k, Bs, -1), 0, 1), h_,
                                       r_.reshape(Bs, G, N), i_.reshape(Bs, G, N), sk, sv)):
            lst.append(val)
    ys = jnp.swapaxes(xs.reshape(Ts, Bs, D), 0, 1)
    return (xp, ys) + tuple(jnp.stack(o) for o in outs)
```

```python
import functools
import math

import jax
import jax.numpy as jnp
from jax import lax
from jax.experimental import pallas as pl
from jax.experimental.pallas import tpu as pltpu

F32 = jnp.float32
BF16 = jnp.bfloat16

RMS_EPS = 1e-6
NEG_INF = -1e30
RG_C = 8.0
ATTN_HEADS = 12
HEAD_DIM = 64
DILATED_GROUPS = ((128, 1), (512, 4), (2048, 16))
W_MAX = max(w for w, _ in DILATED_GROUPS)
Q_BLOCK = 128

LANES = 128
MXU_TILE = 256
VMEM_LIMIT = 56 * 1024 * 1024


def _cparams(sem):
    return pltpu.CompilerParams(dimension_semantics=sem, vmem_limit_bytes=VMEM_LIMIT)


def _rms(x, g):
    return x * lax.rsqrt(jnp.mean(x * x, axis=-1, keepdims=True) + RMS_EPS) * g


def _dot(a, b):
    return jnp.dot(a, b, preferred_element_type=F32)


def _expm1(x):
    e = jnp.exp(x)
    return jnp.where(jnp.abs(x) > 0.5, e - 1.0, jnp.tanh(0.5 * x) * (e + 1.0))


def _const_spec(shape):
    nd = len(shape)
    return pl.BlockSpec(shape, lambda *_: (0,) * nd)


def _in_proj_kernel(splits, tail_from, n_alias, x_ref, g_ref, w_ref, *refs):
    n = len(splits)
    out_refs = refs[n_alias:]
    h = _rms(x_ref[...], g_ref[...]).astype(BF16)
    for (lo, hi), o_ref in zip(splits, out_refs[:n]):
        o_ref[...] = _dot(h, w_ref[:, lo:hi])
    if len(out_refs) > n:
        @pl.when(pl.program_id(1) >= tail_from)
        def _():
            for src, dst in zip(out_refs[n - 2:n], out_refs[n:]):
                dst[...] = src[...].reshape(dst.shape)


def _in_proj(x, g, w, widths, rows, keep=0, layer=0, depth=1, tails=()):
    B, T, D = x.shape
    splits, lo = [], 0
    for wd in widths:
        splits.append((lo, lo + wd))
        lo += wd
    in_specs = [pl.BlockSpec((None, rows, D), lambda b, i: (b, i, 0)), _const_spec((1, D)), _const_spec(w.shape)]
    out_specs = [pl.BlockSpec((None, rows, wd), lambda b, i: (b, i, 0)) for wd in widths]
    out_shape = [jax.ShapeDtypeStruct((B, T, wd), F32) for wd in widths]
    tail_from = (T - keep) // rows
    aliases = {}
    if keep:
        assert keep % rows == 0 and widths[-1] == widths[-2] == ATTN_HEADS * HEAD_DIM
        tail = pl.BlockSpec((None, None, rows, ATTN_HEADS, HEAD_DIM),
                            lambda b, i: (layer, b, jnp.maximum(i - tail_from, 0), 0, 0))
        out_specs += [tail, tail]
        out_shape += [jax.ShapeDtypeStruct((depth, B, keep, ATTN_HEADS, HEAD_DIM), F32)] * 2
        in_specs += [pl.BlockSpec(memory_space=pl.ANY)] * len(tails)
        aliases = {len(in_specs) - len(tails) + j: len(widths) + j for j in range(len(tails))}
    return pl.pallas_call(
        functools.partial(_in_proj_kernel, tuple(splits), tail_from, len(tails)),
        grid=(B, T // rows),
        in_specs=in_specs,
        out_specs=out_specs,
        out_shape=out_shape,
        input_output_aliases=aliases,
        compiler_params=_cparams(("parallel", "arbitrary")),
        name="in_proj",
    )(x, g.reshape(1, D), w, *tails)


def _load_time_major(ref):
    if len(ref.shape) == 2:
        return ref[...]
    nb, tt, w = ref.shape
    return jnp.swapaxes(ref[...], 0, 1).reshape(tt * nb, w)


def _store_time_major(ref, val):
    if len(ref.shape) == 2:
        ref[...] = val.astype(ref.dtype)
    else:
        nb, tt, w = ref.shape
        ref[...] = jnp.swapaxes(val.reshape(tt, nb, w), 0, 1).astype(ref.dtype)


def _seq_specs(x, nb, tt):
    w = x.shape[-1]
    if x.ndim == 3:
        return pl.BlockSpec((nb, tt, w), lambda i: (0, i, 0)), x.shape[1]
    return pl.BlockSpec((tt * nb, w), lambda i: (i, 0)), x.shape[0] // nb


def _rglru_kernel(nb, tt, xa_ref, cs_ref, h0_ref, cw_ref, cb_ref, wr_ref, br_ref, wi_ref, bi_ref, lam_ref,
                  ya_ref, nconv_ref, hlast_ref, xe_buf, a_buf, b_buf, h_carry):
    rows = tt * nb
    halo = cs_ref.shape[0]
    n_tap = cw_ref.shape[0]

    @pl.when(pl.program_id(0) == 0)
    def _():
        xe_buf[0:halo] = cs_ref[...]
        h_carry[...] = h0_ref[...]

    xe_buf[halo:halo + rows] = _load_time_major(xa_ref)
    xc = cb_ref[...] + sum(cw_ref[i:i + 1, :] * xe_buf[i * nb:i * nb + rows] for i in range(n_tap))
    tail = xe_buf[rows:rows + halo]
    nconv_ref[...] = tail
    xe_buf[0:halo] = tail

    xcb = xc.astype(BF16)
    c = -RG_C * jax.nn.softplus(-lam_ref[...])
    for j in range(wr_ref.shape[0]):
        sl = slice(j * MXU_TILE, (j + 1) * MXU_TILE)
        r = jax.nn.sigmoid(_dot(xcb[:, sl], wr_ref[j]) + br_ref[:, sl])
        ig = jax.nn.sigmoid(_dot(xcb[:, sl], wi_ref[j]) + bi_ref[:, sl])
        log_a = c[:, sl] * r
        a_buf[:, sl] = jnp.exp(log_a)
        mult = jnp.sqrt(jnp.maximum(-_expm1(2.0 * log_a), 0.0))
        b_buf[:, sl] = mult * (ig * xc[:, sl])

    def step(t, h):
        rs = pl.ds(pl.multiple_of(t * nb, nb), nb)
        h = a_buf[rs, :] * h + b_buf[rs, :]
        a_buf[rs, :] = h
        return h

    h = lax.fori_loop(0, tt, step, h_carry[...])
    h_carry[...] = h
    hlast_ref[...] = h
    _store_time_major(ya_ref, a_buf[...])


def _rglru(xa, conv_state, h0, p, nb, tt):
    W = xa.shape[-1]
    seq_spec, T = _seq_specs(xa, nb, tt)
    rows = tt * nb
    halo = conv_state.shape[0]
    assert rows >= halo and T % tt == 0
    consts = [p['conv_w'], p['conv_b'], p['wr'], p['br'], p['wi'], p['bi'], p['lam']]
    return pl.pallas_call(
        functools.partial(_rglru_kernel, nb, tt),
        grid=(T // tt,),
        in_specs=[seq_spec, _const_spec(conv_state.shape), _const_spec(h0.shape)]
                 + [_const_spec(a.shape) for a in consts],
        out_specs=[seq_spec, _const_spec(conv_state.shape), _const_spec(h0.shape)],
        out_shape=[jax.ShapeDtypeStruct(xa.shape, BF16), jax.ShapeDtypeStruct(conv_state.shape, F32),
                   jax.ShapeDtypeStruct(h0.shape, F32)],
        scratch_shapes=[pltpu.VMEM((rows + halo, W), F32), pltpu.VMEM((rows, W), F32),
                        pltpu.VMEM((rows, W), F32), pltpu.VMEM((nb, W), F32)],
        compiler_params=_cparams(("arbitrary",)),
        name="rglru",
    )(xa, conv_state, h0, *consts)


def _gelu_tanh(x):
    return 0.5 * x * (1.0 + jnp.tanh(math.sqrt(2.0 / math.pi) * (x + 0.044715 * (x * x * x))))


def _s5_kernel(nb, tt, chunk, u_ref, s0r_ref, s0i_ref, ar_ref, ai_ref, bre_ref, bim_ref, cre_ref, cim_ref,
               d_ref, gw_ref, gb_ref, yb_ref, lastr_ref, lasti_ref, sre, sim, car_r, car_i):
    n_state = sre.shape[1]
    n_half = bre_ref.shape[0]
    kw = bre_ref.shape[1]
    sw = bre_ref.shape[2]
    yw = cre_ref.shape[2]

    @pl.when(pl.program_id(0) == 0)
    def _():
        car_r[...] = s0r_ref[...]
        car_i[...] = s0i_ref[...]

    u = _load_time_major(u_ref)
    ub = u.astype(BF16)
    for hf in range(n_half):
        sre[:, hf * sw:(hf + 1) * sw] = _dot(ub[:, hf * kw:(hf + 1) * kw], bre_ref[hf])
        sim[:, hf * sw:(hf + 1) * sw] = _dot(ub[:, hf * kw:(hf + 1) * kw], bim_ref[hf])

    for c0 in range(0, n_state, chunk):
        cs = slice(c0, c0 + chunk)
        ar = jnp.broadcast_to(ar_ref[:, cs], (nb, chunk))
        ai = jnp.broadcast_to(ai_ref[:, cs], (nb, chunk))

        def step(t, carry):
            s_r, s_i = carry
            rs = pl.ds(pl.multiple_of(t * nb, nb), nb)
            n_r = ar * s_r - ai * s_i + sre[rs, cs]
            n_i = ar * s_i + ai * s_r + sim[rs, cs]
            sre[rs, cs] = n_r
            sim[rs, cs] = n_i
            return n_r, n_i

        s_r, s_i = lax.fori_loop(0, tt, step, (car_r[:, cs], car_i[:, cs]))
        car_r[:, cs] = s_r
        car_i[:, cs] = s_i

    lastr_ref[...] = car_r[...]
    lasti_ref[...] = car_i[...]

    ys = []
    for hf in range(n_half):
        ss = slice(hf * sw, (hf + 1) * sw)
        ys.append(_dot(sre[:, ss].astype(BF16), cre_ref[hf]) - _dot(sim[:, ss].astype(BF16), cim_ref[hf]))
    y = jnp.concatenate(ys, axis=1) + d_ref[...] * u
    z = _gelu_tanh(y).astype(BF16)
    w_half = gw_ref.shape[1] // 2
    val = _dot(z, gw_ref[:, :w_half]) + gb_ref[:, :w_half]
    gate = _dot(z, gw_ref[:, w_half:]) + gb_ref[:, w_half:]
    _store_time_major(yb_ref, val * jax.nn.sigmoid(gate))


def _s5(u, s0r, s0i, p, nb, tt):
    seq_spec, T = _seq_specs(u, nb, tt)
    rows = tt * nb
    n_state = s0r.shape[1]
    chunk = max(LANES, min(n_state, 4 * LANES * 8 // nb))
    consts = [p['a_re'], p['a_im'], p['b_re'], p['b_im'], p['c_re'], p['c_im'], p['d'], p['glu_w'], p['glu_b']]
    return pl.pallas_call(
        functools.partial(_s5_kernel, nb, tt, chunk),
        grid=(T // tt,),
        in_specs=[seq_spec, _const_spec(s0r.shape), _const_spec(s0i.shape)]
                 + [_const_spec(a.shape) for a in consts],
        out_specs=[seq_spec, _const_spec(s0r.shape), _const_spec(s0i.shape)],
        out_shape=[jax.ShapeDtypeStruct(u.shape, BF16), jax.ShapeDtypeStruct(s0r.shape, F32),
                   jax.ShapeDtypeStruct(s0i.shape, F32)],
        scratch_shapes=[pltpu.VMEM((rows, n_state), F32), pltpu.VMEM((rows, n_state), F32),
                        pltpu.VMEM((nb, n_state), F32), pltpu.VMEM((nb, n_state), F32)],
        compiler_params=_cparams(("arbitrary",)),
        name="s5",
    )(u, s0r, s0i, *consts)


def _attn_prompt_kernel(tq, q_ref, kp_ref, kc_ref, vp_ref, vc_ref, o_ref,
                        acc, mst, lst, mask_t, one_hot, sbuf, pbuf):
    it = pl.program_id(2)
    qb = Q_BLOCK
    scale = HEAD_DIM ** -0.5
    lane = lax.broadcasted_iota(jnp.int32, (1, LANES), 1)
    head0 = lane < HEAD_DIM
    key = lax.broadcasted_iota(jnp.int32, (2 * qb, qb), 0)
    row = lax.broadcasted_iota(jnp.int32, (2 * qb, qb), 1)
    band_t = (key >= row) & (key <= row + qb)
    mask_t[0] = jnp.where(band_t, 0.0, NEG_INF).astype(BF16)
    mask_t[1] = jnp.where(band_t & (key >= qb), 0.0, NEG_INF).astype(BF16)
    one_hot[...] = (key % qb == row).astype(BF16)

    blocks = []
    for g, (w, d) in enumerate(DILATED_GROUPS):
        assert w // d == qb
        nblk = tq // (qb * d)
        for idx in range(d * nblk):
            qstart = idx // nblk + d * qb * (idx % nblk)
            kstart = qstart - d * qb
            stride = {} if d == 1 else {"stride": d}
            if kstart < 0:
                ks = (pl.ds(tq + kstart, qb, **stride), pl.ds(qstart, qb, **stride))
            else:
                ks = (pl.ds(kstart, 2 * qb, **stride),)
            blocks.append((g, kstart < 0, pl.ds(qstart, qb, **stride), ks))

    def key_rows(prev_ref, cur_ref, ks):
        if len(ks) == 1:
            return cur_ref[ks[0], :].astype(BF16)
        return jnp.concatenate([prev_ref[ks[0], :], cur_ref[ks[1], :]], axis=0).astype(BF16)

    n_slot = sbuf.shape[0]
    first_tile = (it == 0).astype(jnp.int32)

    def scores(j):
        _, before_tile, qs, ks = blocks[j]
        qt = q_ref[qs, :] * scale
        q2 = jnp.concatenate([jnp.where(head0, qt, 0.0), jnp.where(head0, 0.0, qt)], axis=0).astype(BF16)
        q_aug = jnp.concatenate([q2, one_hot[...]], axis=1)
        k_aug = jnp.concatenate([key_rows(kp_ref, kc_ref, ks), mask_t[first_tile] if before_tile else mask_t[0]],
                                axis=1)
        sbuf[j % n_slot] = lax.dot_general(q_aug, k_aug, (((1,), (1,)), ((), ())), preferred_element_type=F32)

    def probs(j):
        g, _, qs, _ = blocks[j]
        sl = j % n_slot
        m = jnp.max(jnp.maximum(sbuf[sl, :, 0:LANES], sbuf[sl, :, LANES:2 * LANES]), axis=-1, keepdims=True)
        pbuf[sl, :, 0:LANES] = jnp.exp(sbuf[sl, :, 0:LANES] - m).astype(BF16)
        pbuf[sl, :, LANES:2 * LANES] = jnp.exp(sbuf[sl, :, LANES:2 * LANES] - m).astype(BF16)
        mst[g, qs, :] = jnp.where(head0, m[0:qb], m[qb:2 * qb])

    def values(j):
        g, _, qs, ks = blocks[j]
        v_aug = jnp.concatenate([key_rows(vp_ref, vc_ref, ks), jnp.ones((2 * qb, LANES), BF16)], axis=1)
        pv = _dot(pbuf[j % n_slot], v_aug)
        acc[g, qs, :] = jnp.where(head0, pv[0:qb, 0:LANES], pv[qb:2 * qb, 0:LANES])
        lst[g, qs, :] = jnp.where(head0, pv[0:qb, LANES:2 * LANES], pv[qb:2 * qb, LANES:2 * LANES])

    group = 2
    n_chunk = len(blocks) // group
    for step in range(n_chunk + 2):
        @pl.when(it >= 0)
        def _(step=step):
            for c, stage in ((step - 2, values), (step - 1, probs), (step, scores)):
                if 0 <= c < n_chunk:
                    for j in range(c * group, (c + 1) * group):
                        stage(j)

    n_g = len(DILATED_GROUPS)
    m_top = functools.reduce(jnp.maximum, [mst[g] for g in range(n_g)])
    num = 0.0
    den = 0.0
    for g in range(n_g):
        wgt = jnp.exp(mst[g] - m_top)
        num = num + wgt * acc[g]
        den = den + wgt * lst[g]
    o_ref[...] = (num / den).astype(o_ref.dtype)


def _attn_prompt(q, k, v):
    B, T, W = q.shape
    tq = W_MAX
    assert T % tq == 0 and W % LANES == 0
    n_hp = W // LANES
    cur = pl.BlockSpec((None, tq, LANES), lambda b, h, i: (b, i, h))
    prev = pl.BlockSpec((None, tq, LANES), lambda b, h, i: (b, jnp.maximum(i - 1, 0), h))
    n_g = len(DILATED_GROUPS)
    n_blk, qb2 = 12, 2 * Q_BLOCK
    return pl.pallas_call(
        functools.partial(_attn_prompt_kernel, tq),
        grid=(B, n_hp, T // tq),
        in_specs=[cur, prev, cur, prev, cur],
        out_specs=cur,
        out_shape=jax.ShapeDtypeStruct((B, T, W), BF16),
        scratch_shapes=[pltpu.VMEM((n_g, tq, LANES), F32), pltpu.VMEM((n_g, tq, LANES), F32),
                        pltpu.VMEM((n_g, tq, LANES), F32),
                        pltpu.VMEM((2, qb2, Q_BLOCK), BF16), pltpu.VMEM((qb2, Q_BLOCK), BF16),
                        pltpu.VMEM((n_blk, qb2, qb2), F32), pltpu.VMEM((n_blk, qb2, qb2), BF16)],
        compiler_params=_cparams(("parallel", "parallel", "arbitrary")),
        name="attn_prompt",
    )(q, k, k, v, v)


def _attn_sample_kernel(n_new, P, q_ref, ko_ref, kr_ref, vo_ref, vr_ref, kn_ref, vn_ref, o_ref):
    pad, W = q_ref.shape
    n_old, n_res = ko_ref.shape[0], ko_ref.shape[1]
    recent = kr_ref.shape[0]
    d_big = (P - recent) // n_old
    n_rows = ATTN_HEADS * pad
    scale = HEAD_DIM ** -0.5
    q = q_ref[...] * scale
    col_head = lax.broadcasted_iota(jnp.int32, (pad, W), 1) // HEAD_DIM
    qbd = jnp.concatenate([jnp.where(col_head == h, q, 0.0) for h in range(ATTN_HEADS)], axis=0).astype(BF16)

    def multiplicity(dist):
        cnt = jnp.zeros(dist.shape, F32)
        for w, d in DILATED_GROUPS:
            cnt = cnt + ((dist >= 0) & (dist <= w) & (dist % d == 0)).astype(F32)
        return cnt

    def flat(ref):
        return ref[...].reshape(-1, ATTN_HEADS, HEAD_DIM).reshape(-1, W).astype(BF16)

    def dist_to(n_keys, key_pos):
        j = lax.broadcasted_iota(jnp.int32, (n_rows, n_keys), 0) % pad
        c = lax.broadcasted_iota(jnp.int32, (n_rows, n_keys), 1)
        return j + P - key_pos(c), c

    d_old, _ = dist_to(n_old * n_res, lambda c: d_big * (c // n_res) + c % n_res)
    d_rec, _ = dist_to(recent, lambda c: P - recent + c)
    d_new, c_new = dist_to(pad, lambda c: P + c)
    mults = [multiplicity(d_old), multiplicity(d_rec), jnp.where(c_new < n_new, multiplicity(d_new), 0.0)]
    keys = [flat(ko_ref), flat(kr_ref), kn_ref[...].astype(BF16)]
    vals = [flat(vo_ref), flat(vr_ref), vn_ref[...].astype(BF16)]

    nt = (((1,), (1,)), ((), ()))
    scores = [jnp.where(mu > 0, lax.dot_general(qbd, k, nt, preferred_element_type=F32), NEG_INF)
              for k, mu in zip(keys, mults)]
    m = functools.reduce(jnp.maximum, [jnp.max(s, axis=-1, keepdims=True) for s in scores])
    probs = [mu * jnp.exp(s - m) for s, mu in zip(scores, mults)]
    l = sum(jnp.sum(p, axis=-1, keepdims=True) for p in probs)
    pv = sum(_dot(p.astype(BF16), v) for p, v in zip(probs, vals)) / l
    out = jnp.zeros((pad, W), F32)
    for h in range(ATTN_HEADS):
        out = out + jnp.where(col_head == h, pv[h * pad:(h + 1) * pad], 0.0)
    o_ref[...] = out


def _attn_sample(q, k_new, v_new, k_cache, v_cache, layer, n_new):
    B, pad, W = q.shape
    depth, _, P, H, Dh = k_cache.shape
    (recent, _), (w_big, d_big) = sorted(DILATED_GROUPS)[-2:]
    assert P == w_big and n_new <= d_big and (P - recent) % d_big == 0 and P % recent == 0
    n_old = (P - recent) // d_big
    strided = lambda c: c.reshape(depth, B, P // d_big, d_big, H, Dh)
    small = pl.BlockSpec((None, pad, W), lambda b: (b, 0, 0))
    old = pl.BlockSpec((None, None, n_old, n_new, H, Dh), lambda b: (layer, b, 0, 0, 0, 0))
    rec = pl.BlockSpec((None, None, recent, H, Dh), lambda b: (layer, b, P // recent - 1, 0, 0))
    return pl.pallas_call(
        functools.partial(_attn_sample_kernel, n_new, P),
        grid=(B,),
        in_specs=[small, old, rec, old, rec, small, small],
        out_specs=small,
        out_shape=jax.ShapeDtypeStruct((B, pad, W), F32),
        compiler_params=_cparams(("parallel",)),
        name="attn_sample",
    )(q, strided(k_cache), k_cache, strided(v_cache), v_cache, k_new, v_new)


def _merge_kernel(x_ref, gpre_ref, wg_ref, ya_ref, yb_ref, yc_ref, wa_ref, wb_ref, wc_ref, wo_ref, gpost_ref, o_ref):
    x = x_ref[...]
    D = x.shape[1]
    h = _rms(x, gpre_ref[...]).astype(BF16)
    merged = None
    for i, (y_ref, w_ref) in enumerate(((ya_ref, wa_ref), (yb_ref, wb_ref), (yc_ref, wc_ref))):
        gate = jax.nn.sigmoid(_dot(h, wg_ref[:, i * D:(i + 1) * D]))
        term = gate * _dot(y_ref[...].astype(BF16), w_ref[...])
        merged = term if merged is None else merged + term
    t = _dot(merged.astype(BF16), wo_ref[...])
    o_ref[...] = x + _rms(t, gpost_ref[...])


def _merge(x, ya, yb, yc, p, rows):
    B, T, D = x.shape
    spec = lambda a: pl.BlockSpec((None, rows, a.shape[2]), lambda b, i: (b, i, 0))
    xspec = spec(x)
    consts = [p['w_proj_a'], p['w_proj_b'], p['w_proj_c'], p['w_out'], p['norm_post_mix']]
    return pl.pallas_call(
        _merge_kernel,
        grid=(B, T // rows),
        in_specs=[xspec, _const_spec((1, D)), _const_spec(p['w_gates'].shape), spec(ya), spec(yb), spec(yc)]
                 + [_const_spec(a.shape) for a in consts],
        out_specs=xspec,
        out_shape=jax.ShapeDtypeStruct((B, T, D), F32),
        compiler_params=_cparams(("parallel", "parallel")),
        name="merge",
    )(x, p['norm_pre_mix'], p['w_gates'], ya, yb, yc, *consts)


def _mlp_kernel(ff_chunk, x_ref, gpre_ref, w1_ref, w2_ref, gpost_ref, o_ref):
    x = x_ref[...]
    h = _rms(x, gpre_ref[...]).astype(BF16)
    acc = None
    for c0 in range(0, w1_ref.shape[1], ff_chunk):
        hid = jnp.square(jnp.maximum(_dot(h, w1_ref[:, c0:c0 + ff_chunk]), 0.0)).astype(BF16)
        part = _dot(hid, w2_ref[c0:c0 + ff_chunk, :])
        acc = part if acc is None else acc + part
    o_ref[...] = x + _rms(acc, gpost_ref[...])


def _mlp(x, p, rows):
    B, T, D = x.shape
    xspec = pl.BlockSpec((None, rows, D), lambda b, i: (b, i, 0))
    return pl.pallas_call(
        functools.partial(_mlp_kernel, 1024),
        grid=(B, T // rows),
        in_specs=[xspec, _const_spec((1, D)), _const_spec(p['mlp_w1'].shape), _const_spec(p['mlp_w2'].shape),
                  _const_spec((1, D))],
        out_specs=xspec,
        out_shape=jax.ShapeDtypeStruct((B, T, D), F32),
        compiler_params=_cparams(("parallel", "parallel")),
        name="mlp",
    )(x, p['norm_pre_mlp'], p['mlp_w1'], p['mlp_w2'], p['norm_post_mlp'])


def _block_diag_tiles(w, per_tile):
    n, c, d = w.shape
    w = w.reshape(n // per_tile, per_tile, c, d)
    eye = jnp.eye(per_tile, dtype=w.dtype)
    return jnp.einsum('jpcd,pq->jpcqd', w, eye).reshape(n // per_tile, per_tile * c, per_tile * d)


def _layer_params(l, a):
    D = a['w_in'].shape[1]
    lru_w = a['conv_w'].shape[2]
    n_blk, blk = a['lru_wr'].shape[1], a['lru_wr'].shape[2]
    G, N, GC = a['ssm_b_re'].shape[1:]
    ssm_w = G * GC
    attn_w = ATTN_HEADS * HEAD_DIM
    n_front = lru_w + ssm_w + 3 * attn_w
    row = lambda v: v.reshape(1, -1).astype(F32)
    p = {
        'norm_pre_mix': row(a['norm_pre_mix'][l]), 'norm_post_mix': row(a['norm_post_mix'][l]),
        'norm_pre_mlp': row(a['norm_pre_mlp'][l]), 'norm_post_mlp': row(a['norm_post_mlp'][l]),
        'w_front': a['w_in_bf16'][l][:, :n_front], 'w_gates': a['w_in_bf16'][l][:, n_front:],
        'w_proj_a': a['w_proj_a'][l].astype(BF16), 'w_proj_b': a['w_proj_b'][l].astype(BF16),
        'w_proj_c': a['w_proj_c'][l].astype(BF16), 'w_out': a['w_out'][l].astype(BF16),
        'mlp_w1': a['mlp_w1'][l].astype(BF16), 'mlp_w2': a['mlp_w2'][l].astype(BF16),
        'widths': (lru_w, ssm_w, attn_w, attn_w, attn_w),
    }
    per_tile = MXU_TILE // blk
    p['lru'] = {
        'conv_w': a['conv_w'][l].astype(F32), 'conv_b': row(a['conv_b'][l]),
        'wr': _block_diag_tiles(a['lru_wr'][l], per_tile).astype(BF16), 'br': row(a['lru_br'][l]),
        'wi': _block_diag_tiles(a['lru_wi'][l], per_tile).astype(BF16), 'bi': row(a['lru_bi'][l]),
        'lam': row(a['lru_lambda'][l]),
    }
    lr, li = a['ssm_lambda_re'][l].astype(F32), a['ssm_lambda_im'][l].astype(F32)
    dt = jnp.exp(a['ssm_log_dt'][l].astype(F32))[:, None]
    mag = jnp.exp(lr * dt)
    ang = li * dt
    ab_re, ab_im = mag * jnp.cos(ang), mag * jnp.sin(ang)
    den = lr * lr + li * li
    f_re = ((ab_re - 1.0) * lr + ab_im * li) / den
    f_im = (ab_im * lr - (ab_re - 1.0) * li) / den
    b_re, b_im = a['ssm_b_re'][l].astype(F32), a['ssm_b_im'][l].astype(F32)
    bb_re = f_re[..., None] * b_re - f_im[..., None] * b_im
    bb_im = f_re[..., None] * b_im + f_im[..., None] * b_re
    g_half = G // 2
    to_b = lambda m: _block_diag_tiles(jnp.swapaxes(m, 1, 2), g_half).astype(BF16)
    to_c = lambda m: _block_diag_tiles(jnp.swapaxes(m, 1, 2), g_half).astype(BF16)
    p['s5'] = {
        'a_re': ab_re.reshape(1, G * N), 'a_im': ab_im.reshape(1, G * N),
        'b_re': to_b(bb_re), 'b_im': to_b(bb_im),
        'c_re': to_c(a['ssm_c_re'][l].astype(F32)), 'c_im': to_c(a['ssm_c_im'][l].astype(F32)),
        'd': row(a['ssm_d'][l]), 'glu_w': a['glu_w'][l].astype(BF16), 'glu_b': row(a['glu_b'][l]),
    }
    return p


def _tm(state):
    nb, k, w = state.shape
    return jnp.swapaxes(state, 0, 1).reshape(k * nb, w)


def _layer_prompt(x, p, rows, tt, keep, layer, depth, tails):
    B, T, D = x.shape
    lru_w, ssm_w, attn_w = p['widths'][0], p['widths'][1], p['widths'][2]
    xa, u, q, k, v, k_tail, v_tail = _in_proj(x, p['norm_pre_mix'], p['w_front'], p['widths'], rows,
                                              keep, layer, depth, tails)
    n_state = p['s5']['a_re'].shape[1]
    halo = (p['lru']['conv_w'].shape[0] - 1) * B
    ya, nconv, hlast = _rglru(xa, jnp.zeros((halo, lru_w), F32), jnp.zeros((B, lru_w), F32), p['lru'], B, tt)
    yb, s_re, s_im = _s5(u, jnp.zeros((B, n_state), F32), jnp.zeros((B, n_state), F32), p['s5'], B, tt)
    yc = _attn_prompt(q, k, v)
    x = _merge(x, ya, yb, yc, p, rows)
    x = _mlp(x, p, rows)
    return x, nconv, hlast, s_re, s_im, k_tail, v_tail


def _layer_sample(x, conv_state, h0, s0r, s0i, k_cache, v_cache, layer, p, nb, n_new):
    rows = x.shape[1]
    lru_w, ssm_w, attn_w = p['widths'][0], p['widths'][1], p['widths'][2]
    xa, u, q, k, v = _in_proj(x, p['norm_pre_mix'], p['w_front'], p['widths'], rows)
    ya, nconv, hlast = _rglru(xa[0], conv_state, h0, p['lru'], nb, n_new)
    yb, s_re, s_im = _s5(u[0], s0r, s0i, p['s5'], nb, n_new)
    pad = 8

    def bm(arr):
        arr = jnp.swapaxes(arr.reshape(n_new, nb, attn_w), 0, 1)
        return jnp.pad(arr, ((0, 0), (0, pad - n_new), (0, 0)))

    yc = _attn_sample(bm(q), bm(k), bm(v), k_cache, v_cache, layer, n_new)[:, :n_new]
    yc = jnp.swapaxes(yc, 0, 1).reshape(1, rows, attn_w)
    x = _merge(x, ya[None], yb[None], yc, p, rows)
    x = _mlp(x, p, rows)
    return x, nconv, hlast, s_re, s_im, k, v


def kernel(x_prompt, x_sample, state_conv, state_lru, state_ssm_re, state_ssm_im, cache_k, cache_v, norm_pre_mix, norm_post_mix, norm_pre_mlp, norm_post_mlp, w_in, conv_w, conv_b, lru_wr, lru_br, lru_wi, lru_bi, lru_lambda, ssm_lambda_re, ssm_lambda_im, ssm_log_dt, ssm_b_re, ssm_b_im, ssm_c_re, ssm_c_im, ssm_d, glu_w, glu_b, w_proj_a, w_proj_b, w_proj_c, w_out, mlp_w1, mlp_w2):
    a = dict(norm_pre_mix=norm_pre_mix, norm_post_mix=norm_post_mix, norm_pre_mlp=norm_pre_mlp,
             norm_post_mlp=norm_post_mlp, w_in=w_in, conv_w=conv_w, conv_b=conv_b, lru_wr=lru_wr, lru_br=lru_br,
             lru_wi=lru_wi, lru_bi=lru_bi, lru_lambda=lru_lambda, ssm_lambda_re=ssm_lambda_re,
             ssm_lambda_im=ssm_lambda_im, ssm_log_dt=ssm_log_dt, ssm_b_re=ssm_b_re, ssm_b_im=ssm_b_im,
             ssm_c_re=ssm_c_re, ssm_c_im=ssm_c_im, ssm_d=ssm_d, glu_w=glu_w, glu_b=glu_b, w_proj_a=w_proj_a,
             w_proj_b=w_proj_b, w_proj_c=w_proj_c, w_out=w_out, mlp_w1=mlp_w1, mlp_w2=mlp_w2,
             w_in_bf16=w_in.astype(BF16))
    depth = w_in.shape[0]
    Bp, Tp, D = x_prompt.shape
    Bs, Ts, _ = x_sample.shape
    G, N = state_ssm_re.shape[2:]
    attn_w = ATTN_HEADS * HEAD_DIM
    P = cache_k.shape[2]
    assert P == W_MAX, "the window buffer must hold exactly the widest window"
    keep = min(W_MAX, Tp)
    conv_k = state_conv.shape[2]

    xp = x_prompt
    xs = jnp.swapaxes(x_sample, 0, 1).reshape(1, Ts * Bs, D)
    outs = [[] for _ in range(12)]
    tails = ()
    for l in range(depth):
        p = _layer_params(l, a)
        xp, c_, h_, r_, i_, pk, pv = _layer_prompt(xp, p, rows=512, tt=64, keep=keep, layer=l, depth=depth,
                                                   tails=tails)
        tails = (pk, pv)
        for lst, val in zip(outs[:4], (jnp.swapaxes(c_.reshape(conv_k, Bp, -1), 0, 1), h_,
                                       r_.reshape(Bp, G, N), i_.reshape(Bp, G, N))):
            lst.append(val)
        xs, c_, h_, r_, i_, k_, v_ = _layer_sample(
            xs, _tm(state_conv[l]), state_lru[l], state_ssm_re[l].reshape(Bs, G * N),
            state_ssm_im[l].reshape(Bs, G * N), cache_k, cache_v, l, p, Bs, Ts)
        sk = jnp.swapaxes(k_.reshape(Ts, Bs, ATTN_HEADS, HEAD_DIM), 0, 1)
        sv = jnp.swapaxes(v_.reshape(Ts, Bs, ATTN_HEADS, HEAD_DIM), 0, 1)
        for lst, val in zip(outs[6:], (jnp.swapaxes(c_.reshape(conv_k, Bs, -1), 0, 1), h_,
                                       r_.reshape(Bs, G, N), i_.reshape(Bs, G, N), sk, sv)):
            lst.append(val)
    ys = jnp.swapaxes(xs.reshape(Ts, Bs, D), 0, 1)
    stacked = [jnp.stack(o) if o else None for o in outs]
    stacked[4], stacked[5] = tails
    return (xp, ys) + tuple(stacked)
```

```python
import functools
import math

import jax
import jax.numpy as jnp
from jax import lax
from jax.experimental import pallas as pl
from jax.experimental.pallas import tpu as pltpu

F32 = jnp.float32
BF16 = jnp.bfloat16

RMS_EPS = 1e-6
NEG_INF = -1e30
RG_C = 8.0
ATTN_HEADS = 12
HEAD_DIM = 64
DILATED_GROUPS = ((128, 1), (512, 4), (2048, 16))
W_MAX = max(w for w, _ in DILATED_GROUPS)
Q_BLOCK = 128

LANES = 128
MXU_TILE = 256
VMEM_LIMIT = 56 * 1024 * 1024


def _cparams(sem):
    return pltpu.CompilerParams(dimension_semantics=sem, vmem_limit_bytes=VMEM_LIMIT)


def _rms(x, g):
    return x * lax.rsqrt(jnp.mean(x * x, axis=-1, keepdims=True) + RMS_EPS) * g


def _dot(a, b):
    return jnp.dot(a, b, preferred_element_type=F32)


def _neg_expm1_2x(x, ex):
    e2 = ex * ex
    return jnp.where(jnp.abs(x) > 0.25, 1.0 - e2, -jnp.tanh(x) * (e2 + 1.0))


def _sigmoid(x):
    return 0.5 * jnp.tanh(0.5 * x) + 0.5


def _const_spec(shape):
    nd = len(shape)
    return pl.BlockSpec(shape, lambda *_: (0,) * nd)


def _resident_spec(shape):
    nd = len(shape)
    return pl.BlockSpec(shape, lambda *_: (0,) * nd, pipeline_mode=pl.Buffered(1))


def _in_proj_kernel(splits, tail_from, n_alias, x_ref, g_ref, w_ref, *refs):
    n = len(splits)
    out_refs = refs[n_alias:]
    h = _rms(x_ref[...], g_ref[...]).astype(BF16)
    for (lo, hi), o_ref in zip(splits, out_refs[:n]):
        o_ref[...] = _dot(h, w_ref[:, lo:hi])
    if len(out_refs) > n:
        @pl.when(pl.program_id(1) >= tail_from)
        def _():
            for src, dst in zip(out_refs[n - 2:n], out_refs[n:]):
                dst[...] = src[...].reshape(dst.shape)


def _in_proj(x, g, w, widths, rows, keep=0, layer=0, depth=1, tails=()):
    B, T, D = x.shape
    splits, lo = [], 0
    for wd in widths:
        splits.append((lo, lo + wd))
        lo += wd
    in_specs = [pl.BlockSpec((None, rows, D), lambda b, i: (b, i, 0)), _const_spec((1, D)), _const_spec(w.shape)]
    out_specs = [pl.BlockSpec((None, rows, wd), lambda b, i: (b, i, 0)) for wd in widths]
    out_shape = [jax.ShapeDtypeStruct((B, T, wd), F32) for wd in widths]
    tail_from = (T - keep) // rows
    aliases = {}
    if keep:
        assert keep % rows == 0 and widths[-1] == widths[-2] == ATTN_HEADS * HEAD_DIM
        tail = pl.BlockSpec((None, None, rows, ATTN_HEADS, HEAD_DIM),
                            lambda b, i: (layer, b, jnp.maximum(i - tail_from, 0), 0, 0))
        out_specs += [tail, tail]
        out_shape += [jax.ShapeDtypeStruct((depth, B, keep, ATTN_HEADS, HEAD_DIM), F32)] * 2
        in_specs += [pl.BlockSpec(memory_space=pl.ANY)] * len(tails)
        aliases = {len(in_specs) - len(tails) + j: len(widths) + j for j in range(len(tails))}
    return pl.pallas_call(
        functools.partial(_in_proj_kernel, tuple(splits), tail_from, len(tails)),
        grid=(B, T // rows),
        in_specs=in_specs,
        out_specs=out_specs,
        out_shape=out_shape,
        input_output_aliases=aliases,
        compiler_params=_cparams(("parallel", "arbitrary")),
        name="in_proj",
    )(x, g.reshape(1, D), w, *tails)


def _load_time_major(ref):
    if len(ref.shape) == 2:
        return ref[...]
    nb, tt, w = ref.shape
    return jnp.swapaxes(ref[...], 0, 1).reshape(tt * nb, w)


def _store_time_major(ref, val):
    if len(ref.shape) == 2:
        ref[...] = val.astype(ref.dtype)
    else:
        nb, tt, w = ref.shape
        ref[...] = jnp.swapaxes(val.reshape(tt, nb, w), 0, 1).astype(ref.dtype)


def _seq_specs(x, nb, tt):
    w = x.shape[-1]
    if x.ndim == 3:
        return pl.BlockSpec((nb, tt, w), lambda i: (0, i, 0)), x.shape[1]
    return pl.BlockSpec((tt * nb, w), lambda i: (i, 0)), x.shape[0] // nb


def _rglru_kernel(nb, tt, xa_ref, cs_ref, h0_ref, cw_ref, cb_ref, wr_ref, br_ref, wi_ref, bi_ref, lam_ref,
                  ya_ref, nconv_ref, hlast_ref, xe_buf, a_buf, b_buf, h_carry):
    rows = tt * nb
    halo = cs_ref.shape[0]
    n_tap = cw_ref.shape[0]

    @pl.when(pl.program_id(0) == 0)
    def _():
        xe_buf[0:halo] = cs_ref[...]
        h_carry[...] = h0_ref[...]

    xe_buf[halo:halo + rows] = _load_time_major(xa_ref)
    xc = cb_ref[...] + sum(cw_ref[i:i + 1, :] * xe_buf[i * nb:i * nb + rows] for i in range(n_tap))
    tail = xe_buf[rows:rows + halo]
    nconv_ref[...] = tail
    xe_buf[0:halo] = tail

    xcb = xc.astype(BF16)
    c = -RG_C * jax.nn.softplus(-lam_ref[...])
    for j in range(wr_ref.shape[0]):
        sl = slice(j * MXU_TILE, (j + 1) * MXU_TILE)
        r = _sigmoid(_dot(xcb[:, sl], wr_ref[j]) + br_ref[:, sl])
        ig = _sigmoid(_dot(xcb[:, sl], wi_ref[j]) + bi_ref[:, sl])
        log_a = c[:, sl] * r
        a = jnp.exp(log_a)
        a_buf[:, sl] = a
        mult = jnp.sqrt(jnp.maximum(_neg_expm1_2x(log_a, a), 0.0))
        b_buf[:, sl] = mult * (ig * xc[:, sl])

    def step(t, h):
        rs = pl.ds(pl.multiple_of(t * nb, nb), nb)
        h = a_buf[rs, :] * h + b_buf[rs, :]
        a_buf[rs, :] = h
        return h

    h = lax.fori_loop(0, tt, step, h_carry[...])
    h_carry[...] = h
    hlast_ref[...] = h
    _store_time_major(ya_ref, a_buf[...])


def _rglru(xa, conv_state, h0, p, nb, tt):
    W = xa.shape[-1]
    seq_spec, T = _seq_specs(xa, nb, tt)
    rows = tt * nb
    halo = conv_state.shape[0]
    assert rows >= halo and T % tt == 0
    consts = [p['conv_w'], p['conv_b'], p['wr'], p['br'], p['wi'], p['bi'], p['lam']]
    return pl.pallas_call(
        functools.partial(_rglru_kernel, nb, tt),
        grid=(T // tt,),
        in_specs=[seq_spec, _const_spec(conv_state.shape), _const_spec(h0.shape)]
                 + [_const_spec(a.shape) for a in consts],
        out_specs=[seq_spec, _const_spec(conv_state.shape), _const_spec(h0.shape)],
        out_shape=[jax.ShapeDtypeStruct(xa.shape, BF16), jax.ShapeDtypeStruct(conv_state.shape, F32),
                   jax.ShapeDtypeStruct(h0.shape, F32)],
        scratch_shapes=[pltpu.VMEM((rows + halo, W), F32), pltpu.VMEM((rows, W), F32),
                        pltpu.VMEM((rows, W), F32), pltpu.VMEM((nb, W), F32)],
        compiler_params=_cparams(("arbitrary",)),
        name="rglru",
    )(xa, conv_state, h0, *consts)


def _gelu_tanh(x):
    return 0.5 * x * (1.0 + jnp.tanh(math.sqrt(2.0 / math.pi) * (x + 0.044715 * (x * x * x))))


def _s5_kernel(nb, tt, chunk, u_ref, s0r_ref, s0i_ref, ar_ref, ai_ref, bre_ref, bim_ref, cre_ref, cim_ref,
               d_ref, gw_ref, gb_ref, yb_ref, lastr_ref, lasti_ref, sre, sim, car_r, car_i):
    n_state = sre.shape[1]
    n_half = bre_ref.shape[0]
    kw = bre_ref.shape[1]
    sw = bre_ref.shape[2]
    yw = cre_ref.shape[2]

    @pl.when(pl.program_id(0) == 0)
    def _():
        car_r[...] = s0r_ref[...]
        car_i[...] = s0i_ref[...]

    u = _load_time_major(u_ref)
    ub = u.astype(BF16)
    for hf in range(n_half):
        sre[:, hf * sw:(hf + 1) * sw] = _dot(ub[:, hf * kw:(hf + 1) * kw], bre_ref[hf])
        sim[:, hf * sw:(hf + 1) * sw] = _dot(ub[:, hf * kw:(hf + 1) * kw], bim_ref[hf])

    for c0 in range(0, n_state, chunk):
        cs = slice(c0, c0 + chunk)
        ar = jnp.broadcast_to(ar_ref[:, cs], (nb, chunk))
        ai = jnp.broadcast_to(ai_ref[:, cs], (nb, chunk))

        def step(t, carry):
            s_r, s_i = carry
            rs = pl.ds(pl.multiple_of(t * nb, nb), nb)
            n_r = ar * s_r - ai * s_i + sre[rs, cs]
            n_i = ar * s_i + ai * s_r + sim[rs, cs]
            sre[rs, cs] = n_r
            sim[rs, cs] = n_i
            return n_r, n_i

        s_r, s_i = lax.fori_loop(0, tt, step, (car_r[:, cs], car_i[:, cs]))
        car_r[:, cs] = s_r
        car_i[:, cs] = s_i

    lastr_ref[...] = car_r[...]
    lasti_ref[...] = car_i[...]

    ys = []
    for hf in range(n_half):
        ss = slice(hf * sw, (hf + 1) * sw)
        ys.append(_dot(sre[:, ss].astype(BF16), cre_ref[hf]) - _dot(sim[:, ss].astype(BF16), cim_ref[hf]))
    y = jnp.concatenate(ys, axis=1) + d_ref[...] * u
    z = _gelu_tanh(y).astype(BF16)
    w_half = gw_ref.shape[1] // 2
    val = _dot(z, gw_ref[:, :w_half]) + gb_ref[:, :w_half]
    gate = _dot(z, gw_ref[:, w_half:]) + gb_ref[:, w_half:]
    _store_time_major(yb_ref, val * jax.nn.sigmoid(gate))


def _s5(u, s0r, s0i, p, nb, tt):
    seq_spec, T = _seq_specs(u, nb, tt)
    rows = tt * nb
    n_state = s0r.shape[1]
    chunk = max(LANES, min(n_state, 4 * LANES * 8 // nb))
    consts = [p['a_re'], p['a_im'], p['b_re'], p['b_im'], p['c_re'], p['c_im'], p['d'], p['glu_w'], p['glu_b']]
    return pl.pallas_call(
        functools.partial(_s5_kernel, nb, tt, chunk),
        grid=(T // tt,),
        in_specs=[seq_spec, _const_spec(s0r.shape), _const_spec(s0i.shape)]
                 + [_const_spec(a.shape) for a in consts],
        out_specs=[seq_spec, _const_spec(s0r.shape), _const_spec(s0i.shape)],
        out_shape=[jax.ShapeDtypeStruct(u.shape, BF16), jax.ShapeDtypeStruct(s0r.shape, F32),
                   jax.ShapeDtypeStruct(s0i.shape, F32)],
        scratch_shapes=[pltpu.VMEM((rows, n_state), F32), pltpu.VMEM((rows, n_state), F32),
                        pltpu.VMEM((nb, n_state), F32), pltpu.VMEM((nb, n_state), F32)],
        compiler_params=_cparams(("arbitrary",)),
        name="s5",
    )(u, s0r, s0i, *consts)


def _attn_prompt_kernel(tq, q_ref, kp_ref, kc_ref, vp_ref, vc_ref, o_ref,
                        acc, mst, lst, mask_t, one_hot, sbuf, pbuf):
    it = pl.program_id(2)
    qb = Q_BLOCK
    scale = HEAD_DIM ** -0.5
    lane = lax.broadcasted_iota(jnp.int32, (1, LANES), 1)
    head0 = lane < HEAD_DIM
    key = lax.broadcasted_iota(jnp.int32, (2 * qb, qb), 0)
    row = lax.broadcasted_iota(jnp.int32, (2 * qb, qb), 1)
    band_t = (key >= row) & (key <= row + qb)
    mask_t[0] = jnp.where(band_t, 0.0, NEG_INF).astype(BF16)
    mask_t[1] = jnp.where(band_t & (key >= qb), 0.0, NEG_INF).astype(BF16)
    one_hot[...] = (key % qb == row).astype(BF16)

    blocks = []
    for g, (w, d) in enumerate(DILATED_GROUPS):
        assert w // d == qb
        nblk = tq // (qb * d)
        for idx in range(d * nblk):
            qstart = idx // nblk + d * qb * (idx % nblk)
            kstart = qstart - d * qb
            stride = {} if d == 1 else {"stride": d}
            if kstart < 0:
                ks = (pl.ds(tq + kstart, qb, **stride), pl.ds(qstart, qb, **stride))
            else:
                ks = (pl.ds(kstart, 2 * qb, **stride),)
            blocks.append((g, kstart < 0, pl.ds(qstart, qb, **stride), ks))

    def key_rows(prev_ref, cur_ref, ks):
        if len(ks) == 1:
            return cur_ref[ks[0], :].astype(BF16)
        return jnp.concatenate([prev_ref[ks[0], :], cur_ref[ks[1], :]], axis=0).astype(BF16)

    n_slot = sbuf.shape[0]
    first_tile = (it == 0).astype(jnp.int32)

    def scores(j):
        _, before_tile, qs, ks = blocks[j]
        qt = q_ref[qs, :] * scale
        q2 = jnp.concatenate([jnp.where(head0, qt, 0.0), jnp.where(head0, 0.0, qt)], axis=0).astype(BF16)
        q_aug = jnp.concatenate([q2, one_hot[...]], axis=1)
        k_aug = jnp.concatenate([key_rows(kp_ref, kc_ref, ks), mask_t[first_tile] if before_tile else mask_t[0]],
                                axis=1)
        sbuf[j % n_slot] = lax.dot_general(q_aug, k_aug, (((1,), (1,)), ((), ())), preferred_element_type=F32)

    def probs(j):
        g, _, qs, _ = blocks[j]
        sl = j % n_slot
        m = jnp.max(jnp.maximum(sbuf[sl, :, 0:LANES], sbuf[sl, :, LANES:2 * LANES]), axis=-1, keepdims=True)
        pbuf[sl, :, 0:LANES] = jnp.exp(sbuf[sl, :, 0:LANES] - m).astype(BF16)
        pbuf[sl, :, LANES:2 * LANES] = jnp.exp(sbuf[sl, :, LANES:2 * LANES] - m).astype(BF16)
        mst[g, qs, :] = jnp.where(head0, m[0:qb], m[qb:2 * qb])

    def values(j):
        g, _, qs, ks = blocks[j]
        v_aug = jnp.concatenate([key_rows(vp_ref, vc_ref, ks), jnp.ones((2 * qb, LANES), BF16)], axis=1)
        pv = _dot(pbuf[j % n_slot], v_aug)
        acc[g, qs, :] = jnp.where(head0, pv[0:qb, 0:LANES], pv[qb:2 * qb, 0:LANES])
        lst[g, qs, :] = jnp.where(head0, pv[0:qb, LANES:2 * LANES], pv[qb:2 * qb, LANES:2 * LANES])

    group = 2
    n_chunk = len(blocks) // group
    for step in range(n_chunk + 2):
        @pl.when(it >= 0)
        def _(step=step):
            for c, stage in ((step - 2, values), (step - 1, probs), (step, scores)):
                if 0 <= c < n_chunk:
                    for j in range(c * group, (c + 1) * group):
                        stage(j)

    n_g = len(DILATED_GROUPS)
    m_top = functools.reduce(jnp.maximum, [mst[g] for g in range(n_g)])
    num = 0.0
    den = 0.0
    for g in range(n_g):
        wgt = jnp.exp(mst[g] - m_top)
        num = num + wgt * acc[g]
        den = den + wgt * lst[g]
    o_ref[...] = (num / den).astype(o_ref.dtype)


def _attn_prompt(q, k, v):
    B, T, W = q.shape
    tq = W_MAX
    assert T % tq == 0 and W % LANES == 0
    n_hp = W // LANES
    cur = pl.BlockSpec((None, tq, LANES), lambda b, h, i: (b, i, h))
    prev = pl.BlockSpec((None, tq, LANES), lambda b, h, i: (b, jnp.maximum(i - 1, 0), h))
    n_g = len(DILATED_GROUPS)
    n_blk, qb2 = 12, 2 * Q_BLOCK
    return pl.pallas_call(
        functools.partial(_attn_prompt_kernel, tq),
        grid=(B, n_hp, T // tq),
        in_specs=[cur, prev, cur, prev, cur],
        out_specs=cur,
        out_shape=jax.ShapeDtypeStruct((B, T, W), BF16),
        scratch_shapes=[pltpu.VMEM((n_g, tq, LANES), F32), pltpu.VMEM((n_g, tq, LANES), F32),
                        pltpu.VMEM((n_g, tq, LANES), F32),
                        pltpu.VMEM((2, qb2, Q_BLOCK), BF16), pltpu.VMEM((qb2, Q_BLOCK), BF16),
                        pltpu.VMEM((n_blk, qb2, qb2), F32), pltpu.VMEM((n_blk, qb2, qb2), BF16)],
        compiler_params=_cparams(("parallel", "parallel", "arbitrary")),
        name="attn_prompt",
    )(q, k, k, v, v)


def _attn_sample_kernel(n_new, q_ref, kc_ref, vc_ref, kn_ref, vn_ref, o_ref):
    P = kc_ref.shape[0]
    W = kc_ref.shape[1]
    pad = q_ref.shape[0]
    n_rows = ATTN_HEADS * pad
    scale = HEAD_DIM ** -0.5
    q = q_ref[...]
    col_head = lax.broadcasted_iota(jnp.int32, (pad, W), 1) // HEAD_DIM
    qbd = jnp.concatenate([jnp.where(col_head == h, q, 0.0) for h in range(ATTN_HEADS)], axis=0).astype(BF16)

    def multiplicity(dist):
        cnt = jnp.zeros(dist.shape, F32)
        for w, d in DILATED_GROUPS:
            cnt = cnt + ((dist >= 0) & (dist <= w) & (dist % d == 0)).astype(F32)
        return cnt

    nt = (((1,), (1,)), ((), ()))
    s_c = lax.dot_general(qbd, kc_ref[...].astype(BF16), nt, preferred_element_type=F32) * scale
    s_n = lax.dot_general(qbd, kn_ref[...].astype(BF16), nt, preferred_element_type=F32) * scale
    j_c = lax.broadcasted_iota(jnp.int32, (n_rows, P), 0) % pad
    i_c = lax.broadcasted_iota(jnp.int32, (n_rows, P), 1)
    mult_c = multiplicity(j_c + P - i_c)
    j_n = lax.broadcasted_iota(jnp.int32, (n_rows, pad), 0) % pad
    i_n = lax.broadcasted_iota(jnp.int32, (n_rows, pad), 1)
    mult_n = jnp.where(i_n < n_new, multiplicity(j_n - i_n), 0.0)
    s_c = jnp.where(mult_c > 0, s_c, NEG_INF)
    s_n = jnp.where(mult_n > 0, s_n, NEG_INF)
    m = jnp.maximum(jnp.max(s_c, axis=-1, keepdims=True), jnp.max(s_n, axis=-1, keepdims=True))
    p_c = mult_c * jnp.exp(s_c - m)
    p_n = mult_n * jnp.exp(s_n - m)
    l = jnp.sum(p_c, axis=-1, keepdims=True) + jnp.sum(p_n, axis=-1, keepdims=True)
    pv = (_dot(p_c.astype(BF16), vc_ref[...].astype(BF16)) + _dot(p_n.astype(BF16), vn_ref[...].astype(BF16))) / l
    out = jnp.zeros((pad, W), F32)
    for h in range(ATTN_HEADS):
        out = out + jnp.where(col_head == h, pv[h * pad:(h + 1) * pad], 0.0)
    o_ref[...] = out


def _attn_sample(q, k_new, v_new, k_cache, v_cache, n_new):
    B, pad, W = q.shape
    P = k_cache.shape[1]
    small = pl.BlockSpec((None, pad, W), lambda b: (b, 0, 0))
    big = pl.BlockSpec((None, P, W), lambda b: (b, 0, 0))
    return pl.pallas_call(
        functools.partial(_attn_sample_kernel, n_new),
        grid=(B,),
        in_specs=[small, big, big, small, small],
        out_specs=small,
        out_shape=jax.ShapeDtypeStruct((B, pad, W), F32),
        compiler_params=_cparams(("parallel",)),
        name="attn_sample",
    )(q, k_cache, v_cache, k_new, v_new)


def _mix_mlp_kernel(ff_chunk, x_ref, gpre_ref, wg_ref, ya_ref, yb_ref, yc_ref, wa_ref, wb_ref, wc_ref, wo_ref,
                    gpost_ref, g2pre_ref, w1_ref, w2_ref, g2post_ref, o_ref):
    x = x_ref[...]
    D = x.shape[1]
    h = _rms(x, gpre_ref[...]).astype(BF16)
    merged = None
    for i, (y_ref, w_ref) in enumerate(((ya_ref, wa_ref), (yb_ref, wb_ref), (yc_ref, wc_ref))):
        gate = jax.nn.sigmoid(_dot(h, wg_ref[:, i * D:(i + 1) * D]))
        term = gate * _dot(y_ref[...].astype(BF16), w_ref[...])
        merged = term if merged is None else merged + term
    x = x + _rms(_dot(merged.astype(BF16), wo_ref[...]), gpost_ref[...])

    h = _rms(x, g2pre_ref[...]).astype(BF16)
    acc = None
    for c0 in range(0, w1_ref.shape[1], ff_chunk):
        hid = jnp.square(jnp.maximum(_dot(h, w1_ref[:, c0:c0 + ff_chunk]), 0.0)).astype(BF16)
        part = _dot(hid, w2_ref[c0:c0 + ff_chunk, :])
        acc = part if acc is None else acc + part
    o_ref[...] = x + _rms(acc, g2post_ref[...])


def _mix_mlp(x, ya, yb, yc, p, rows):
    B, T, D = x.shape
    spec = lambda a: pl.BlockSpec((None, rows, a.shape[2]), lambda b, i: (b, i, 0))
    consts = [p['norm_pre_mix'], p['w_gates']]
    tail = [p['w_proj_a'], p['w_proj_b'], p['w_proj_c'], p['w_out'], p['norm_post_mix'],
            p['norm_pre_mlp'], p['mlp_w1'], p['mlp_w2'], p['norm_post_mlp']]
    return pl.pallas_call(
        functools.partial(_mix_mlp_kernel, 1024),
        grid=(B, T // rows),
        in_specs=[spec(x)] + [_resident_spec(a.shape) for a in consts] + [spec(ya), spec(yb), spec(yc)]
                 + [_resident_spec(a.shape) for a in tail],
        out_specs=spec(x),
        out_shape=jax.ShapeDtypeStruct((B, T, D), F32),
        compiler_params=_cparams(("parallel", "parallel")),
        name="mix_mlp",
    )(x, *consts, ya, yb, yc, *tail)


def _block_diag_tiles(w, per_tile):
    n, c, d = w.shape
    w = w.reshape(n // per_tile, per_tile, c, d)
    eye = jnp.eye(per_tile, dtype=w.dtype)
    return jnp.einsum('jpcd,pq->jpcqd', w, eye).reshape(n // per_tile, per_tile * c, per_tile * d)


def _layer_params(l, a):
    D = a['w_in'].shape[1]
    lru_w = a['conv_w'].shape[2]
    n_blk, blk = a['lru_wr'].shape[1], a['lru_wr'].shape[2]
    G, N, GC = a['ssm_b_re'].shape[1:]
    ssm_w = G * GC
    attn_w = ATTN_HEADS * HEAD_DIM
    n_front = lru_w + ssm_w + 3 * attn_w
    row = lambda v: v.reshape(1, -1).astype(F32)
    p = {
        'norm_pre_mix': row(a['norm_pre_mix'][l]), 'norm_post_mix': row(a['norm_post_mix'][l]),
        'norm_pre_mlp': row(a['norm_pre_mlp'][l]), 'norm_post_mlp': row(a['norm_post_mlp'][l]),
        'w_front': a['w_in_bf16'][l][:, :n_front], 'w_gates': a['w_in_bf16'][l][:, n_front:],
        'w_proj_a': a['w_proj_a'][l].astype(BF16), 'w_proj_b': a['w_proj_b'][l].astype(BF16),
        'w_proj_c': a['w_proj_c'][l].astype(BF16), 'w_out': a['w_out'][l].astype(BF16),
        'mlp_w1': a['mlp_w1'][l].astype(BF16), 'mlp_w2': a['mlp_w2'][l].astype(BF16),
        'widths': (lru_w, ssm_w, attn_w, attn_w, attn_w),
    }
    per_tile = MXU_TILE // blk
    p['lru'] = {
        'conv_w': a['conv_w'][l].astype(F32), 'conv_b': row(a['conv_b'][l]),
        'wr': _block_diag_tiles(a['lru_wr'][l], per_tile).astype(BF16), 'br': row(a['lru_br'][l]),
        'wi': _block_diag_tiles(a['lru_wi'][l], per_tile).astype(BF16), 'bi': row(a['lru_bi'][l]),
        'lam': row(a['lru_lambda'][l]),
    }
    lr, li = a['ssm_lambda_re'][l].astype(F32), a['ssm_lambda_im'][l].astype(F32)
    dt = jnp.exp(a['ssm_log_dt'][l].astype(F32))[:, None]
    mag = jnp.exp(lr * dt)
    ang = li * dt
    ab_re, ab_im = mag * jnp.cos(ang), mag * jnp.sin(ang)
    den = lr * lr + li * li
    f_re = ((ab_re - 1.0) * lr + ab_im * li) / den
    f_im = (ab_im * lr - (ab_re - 1.0) * li) / den
    b_re, b_im = a['ssm_b_re'][l].astype(F32), a['ssm_b_im'][l].astype(F32)
    bb_re = f_re[..., None] * b_re - f_im[..., None] * b_im
    bb_im = f_re[..., None] * b_im + f_im[..., None] * b_re
    g_half = G // 2
    to_b = lambda m: _block_diag_tiles(jnp.swapaxes(m, 1, 2), g_half).astype(BF16)
    to_c = lambda m: _block_diag_tiles(jnp.swapaxes(m, 1, 2), g_half).astype(BF16)
    p['s5'] = {
        'a_re': ab_re.reshape(1, G * N), 'a_im': ab_im.reshape(1, G * N),
        'b_re': to_b(bb_re), 'b_im': to_b(bb_im),
        'c_re': to_c(a['ssm_c_re'][l].astype(F32)), 'c_im': to_c(a['ssm_c_im'][l].astype(F32)),
        'd': row(a['ssm_d'][l]), 'glu_w': a['glu_w'][l].astype(BF16), 'glu_b': row(a['glu_b'][l]),
    }
    return p


def _tm(state):
    nb, k, w = state.shape
    return jnp.swapaxes(state, 0, 1).reshape(k * nb, w)


def _layer_prompt(x, p, rows, tt, keep, layer, depth, tails):
    B, T, D = x.shape
    lru_w, ssm_w, attn_w = p['widths'][0], p['widths'][1], p['widths'][2]
    xa, u, q, k, v, k_tail, v_tail = _in_proj(x, p['norm_pre_mix'], p['w_front'], p['widths'], rows,
                                              keep, layer, depth, tails)
    n_state = p['s5']['a_re'].shape[1]
    halo = (p['lru']['conv_w'].shape[0] - 1) * B
    ya, nconv, hlast = _rglru(xa, jnp.zeros((halo, lru_w), F32), jnp.zeros((B, lru_w), F32), p['lru'], B, tt)
    yb, s_re, s_im = _s5(u, jnp.zeros((B, n_state), F32), jnp.zeros((B, n_state), F32), p['s5'], B, tt)
    yc = _attn_prompt(q, k, v)
    x = _mix_mlp(x, ya, yb, yc, p, rows)
    return x, nconv, hlast, s_re, s_im, k_tail, v_tail


def _layer_sample(x, conv_state, h0, s0r, s0i, k_cache, v_cache, p, nb, n_new):
    rows = x.shape[1]
    lru_w, ssm_w, attn_w = p['widths'][0], p['widths'][1], p['widths'][2]
    xa, u, q, k, v = _in_proj(x, p['norm_pre_mix'], p['w_front'], p['widths'], rows)
    ya, nconv, hlast = _rglru(xa[0], conv_state, h0, p['lru'], nb, n_new)
    yb, s_re, s_im = _s5(u[0], s0r, s0i, p['s5'], nb, n_new)
    pad = 8

    def bm(arr):
        arr = jnp.swapaxes(arr.reshape(n_new, nb, attn_w), 0, 1)
        return jnp.pad(arr, ((0, 0), (0, pad - n_new), (0, 0)))

    yc = _attn_sample(bm(q), bm(k), bm(v), k_cache, v_cache, n_new)[:, :n_new]
    yc = jnp.swapaxes(yc, 0, 1).reshape(1, rows, attn_w)
    x = _mix_mlp(x, ya[None], yb[None], yc, p, rows)
    return x, nconv, hlast, s_re, s_im, k, v


def kernel(x_prompt, x_sample, state_conv, state_lru, state_ssm_re, state_ssm_im, cache_k, cache_v, norm_pre_mix, norm_post_mix, norm_pre_mlp, norm_post_mlp, w_in, conv_w, conv_b, lru_wr, lru_br, lru_wi, lru_bi, lru_lambda, ssm_lambda_re, ssm_lambda_im, ssm_log_dt, ssm_b_re, ssm_b_im, ssm_c_re, ssm_c_im, ssm_d, glu_w, glu_b, w_proj_a, w_proj_b, w_proj_c, w_out, mlp_w1, mlp_w2):
    a = dict(norm_pre_mix=norm_pre_mix, norm_post_mix=norm_post_mix, norm_pre_mlp=norm_pre_mlp,
             norm_post_mlp=norm_post_mlp, w_in=w_in, conv_w=conv_w, conv_b=conv_b, lru_wr=lru_wr, lru_br=lru_br,
             lru_wi=lru_wi, lru_bi=lru_bi, lru_lambda=lru_lambda, ssm_lambda_re=ssm_lambda_re,
             ssm_lambda_im=ssm_lambda_im, ssm_log_dt=ssm_log_dt, ssm_b_re=ssm_b_re, ssm_b_im=ssm_b_im,
             ssm_c_re=ssm_c_re, ssm_c_im=ssm_c_im, ssm_d=ssm_d, glu_w=glu_w, glu_b=glu_b, w_proj_a=w_proj_a,
             w_proj_b=w_proj_b, w_proj_c=w_proj_c, w_out=w_out, mlp_w1=mlp_w1, mlp_w2=mlp_w2,
             w_in_bf16=w_in.astype(BF16))
    depth = w_in.shape[0]
    Bp, Tp, D = x_prompt.shape
    Bs, Ts, _ = x_sample.shape
    G, N = state_ssm_re.shape[2:]
    attn_w = ATTN_HEADS * HEAD_DIM
    P = cache_k.shape[2]
    assert P == W_MAX, "the window buffer must hold exactly the widest window"
    keep = min(W_MAX, Tp)
    conv_k = state_conv.shape[2]

    xp = x_prompt
    xs = jnp.swapaxes(x_sample, 0, 1).reshape(1, Ts * Bs, D)
    outs = [[] for _ in range(12)]
    tails = ()
    for l in range(depth):
        p = _layer_params(l, a)
        xp, c_, h_, r_, i_, pk, pv = _layer_prompt(xp, p, rows=512, tt=64, keep=keep, layer=l, depth=depth,
                                                   tails=tails)
        tails = (pk, pv)
        for lst, val in zip(outs[:4], (jnp.swapaxes(c_.reshape(conv_k, Bp, -1), 0, 1), h_,
                                       r_.reshape(Bp, G, N), i_.reshape(Bp, G, N))):
            lst.append(val)
        xs, c_, h_, r_, i_, k_, v_ = _layer_sample(
            xs, _tm(state_conv[l]), state_lru[l], state_ssm_re[l].reshape(Bs, G * N),
            state_ssm_im[l].reshape(Bs, G * N), cache_k[l].reshape(Bs, P, attn_w).astype(BF16),
            cache_v[l].reshape(Bs, P, attn_w).astype(BF16), p, Bs, Ts)
        sk = jnp.swapaxes(k_.reshape(Ts, Bs, ATTN_HEADS, HEAD_DIM), 0, 1)
        sv = jnp.swapaxes(v_.reshape(Ts, Bs, ATTN_HEADS, HEAD_DIM), 0, 1)
        for lst, val in zip(outs[6:], (jnp.swapaxes(c_.reshape(conv_k, Bs, -1), 0, 1), h_,
                                       r_.reshape(Bs, G, N), i_.reshape(Bs, G, N), sk, sv)):
            lst.append(val)
    ys = jnp.swapaxes(xs.reshape(Ts, Bs, D), 0, 1)
    stacked = [jnp.stack(o) if o else None for o in outs]
    stacked[4], stacked[5] = tails
    return (xp, ys) + tuple(stacked)
```

```python
import functools
import math

import jax
import jax.numpy as jnp
from jax import lax
from jax.experimental import pallas as pl
from jax.experimental.pallas import tpu as pltpu

F32 = jnp.float32
BF16 = jnp.bfloat16

RMS_EPS = 1e-6
NEG_INF = -1e30
RG_C = 8.0
ATTN_HEADS = 12
HEAD_DIM = 64
DILATED_GROUPS = ((128, 1), (512, 4), (2048, 16))
W_MAX = max(w for w, _ in DILATED_GROUPS)
Q_BLOCK = 128

LANES = 128
MXU_TILE = 256
VMEM_LIMIT = 56 * 1024 * 1024


def _cparams(sem):
    return pltpu.CompilerParams(dimension_semantics=sem, vmem_limit_bytes=VMEM_LIMIT)


def _rms(x, g):
    return x * lax.rsqrt(jnp.mean(x * x, axis=-1, keepdims=True) + RMS_EPS) * g


def _dot(a, b):
    return jnp.dot(a, b, preferred_element_type=F32)


def _neg_expm1_2x(x, ex):
    e2 = ex * ex
    return jnp.where(jnp.abs(x) > 0.25, 1.0 - e2, -jnp.tanh(x) * (e2 + 1.0))


def _sigmoid(x):
    return 0.5 * jnp.tanh(0.5 * x) + 0.5


def _const_spec(shape):
    nd = len(shape)
    return pl.BlockSpec(shape, lambda *_: (0,) * nd)


def _resident_spec(shape):
    nd = len(shape)
    return pl.BlockSpec(shape, lambda *_: (0,) * nd, pipeline_mode=pl.Buffered(1))


def _in_proj_kernel(splits, tail_from, n_alias, x_ref, g_ref, w_ref, *refs):
    n = len(splits)
    out_refs = refs[n_alias:]
    h = _rms(x_ref[...], g_ref[...]).astype(BF16)
    for (lo, hi), o_ref in zip(splits, out_refs[:n]):
        o_ref[...] = _dot(h, w_ref[:, lo:hi])
    if len(out_refs) > n:
        @pl.when(pl.program_id(1) >= tail_from)
        def _():
            for src, dst in zip(out_refs[n - 2:n], out_refs[n:]):
                dst[...] = src[...].reshape(dst.shape)


def _in_proj(x, g, w, widths, rows, keep=0, layer=0, depth=1, tails=()):
    B, T, D = x.shape
    splits, lo = [], 0
    for wd in widths:
        splits.append((lo, lo + wd))
        lo += wd
    in_specs = [pl.BlockSpec((None, rows, D), lambda b, i: (b, i, 0)), _const_spec((1, D)), _const_spec(w.shape)]
    out_specs = [pl.BlockSpec((None, rows, wd), lambda b, i: (b, i, 0)) for wd in widths]
    out_shape = [jax.ShapeDtypeStruct((B, T, wd), F32) for wd in widths]
    tail_from = (T - keep) // rows
    aliases = {}
    if keep:
        assert keep % rows == 0 and widths[-1] == widths[-2] == ATTN_HEADS * HEAD_DIM
        tail = pl.BlockSpec((None, None, rows, ATTN_HEADS, HEAD_DIM),
                            lambda b, i: (layer, b, jnp.maximum(i - tail_from, 0), 0, 0))
        out_specs += [tail, tail]
        out_shape += [jax.ShapeDtypeStruct((depth, B, keep, ATTN_HEADS, HEAD_DIM), F32)] * 2
        in_specs += [pl.BlockSpec(memory_space=pl.ANY)] * len(tails)
        aliases = {len(in_specs) - len(tails) + j: len(widths) + j for j in range(len(tails))}
    return pl.pallas_call(
        functools.partial(_in_proj_kernel, tuple(splits), tail_from, len(tails)),
        grid=(B, T // rows),
        in_specs=in_specs,
        out_specs=out_specs,
        out_shape=out_shape,
        input_output_aliases=aliases,
        compiler_params=_cparams(("parallel", "arbitrary")),
        name="in_proj",
    )(x, g.reshape(1, D), w, *tails)


def _load_time_major(ref):
    if len(ref.shape) == 2:
        return ref[...]
    nb, tt, w = ref.shape
    return jnp.swapaxes(ref[...], 0, 1).reshape(tt * nb, w)


def _store_time_major(ref, val):
    if len(ref.shape) == 2:
        ref[...] = val.astype(ref.dtype)
    else:
        nb, tt, w = ref.shape
        ref[...] = jnp.swapaxes(val.reshape(tt, nb, w), 0, 1).astype(ref.dtype)


def _seq_specs(x, nb, tt):
    w = x.shape[-1]
    if x.ndim == 3:
        return pl.BlockSpec((nb, tt, w), lambda i: (0, i, 0)), x.shape[1]
    return pl.BlockSpec((tt * nb, w), lambda i: (i, 0)), x.shape[0] // nb


def _rglru_kernel(nb, tt, xa_ref, cs_ref, h0_ref, cw_ref, cb_ref, wr_ref, br_ref, wi_ref, bi_ref, lam_ref,
                  ya_ref, nconv_ref, hlast_ref, xe_buf, a_buf, b_buf, h_carry):
    rows = tt * nb
    halo = cs_ref.shape[0]
    n_tap = cw_ref.shape[0]

    @pl.when(pl.program_id(0) == 0)
    def _():
        xe_buf[0:halo] = cs_ref[...]
        h_carry[...] = h0_ref[...]

    xe_buf[halo:halo + rows] = _load_time_major(xa_ref)
    xc = cb_ref[...] + sum(cw_ref[i:i + 1, :] * xe_buf[i * nb:i * nb + rows] for i in range(n_tap))
    tail = xe_buf[rows:rows + halo]
    nconv_ref[...] = tail
    xe_buf[0:halo] = tail

    xcb = xc.astype(BF16)
    c = -RG_C * jax.nn.softplus(-lam_ref[...])
    for j in range(wr_ref.shape[0]):
        sl = slice(j * MXU_TILE, (j + 1) * MXU_TILE)
        r = _sigmoid(_dot(xcb[:, sl], wr_ref[j]) + br_ref[:, sl])
        ig = _sigmoid(_dot(xcb[:, sl], wi_ref[j]) + bi_ref[:, sl])
        log_a = c[:, sl] * r
        a = jnp.exp(log_a)
        a_buf[:, sl] = a
        mult = jnp.sqrt(jnp.maximum(_neg_expm1_2x(log_a, a), 0.0))
        b_buf[:, sl] = mult * (ig * xc[:, sl])

    def step(t, h):
        rs = pl.ds(pl.multiple_of(t * nb, nb), nb)
        h = a_buf[rs, :] * h + b_buf[rs, :]
        a_buf[rs, :] = h
        return h

    h = lax.fori_loop(0, tt, step, h_carry[...], unroll=True)
    h_carry[...] = h
    hlast_ref[...] = h
    _store_time_major(ya_ref, a_buf[...])


def _rglru(xa, conv_state, h0, p, nb, tt):
    W = xa.shape[-1]
    seq_spec, T = _seq_specs(xa, nb, tt)
    rows = tt * nb
    halo = conv_state.shape[0]
    assert rows >= halo and T % tt == 0
    consts = [p['conv_w'], p['conv_b'], p['wr'], p['br'], p['wi'], p['bi'], p['lam']]
    return pl.pallas_call(
        functools.partial(_rglru_kernel, nb, tt),
        grid=(T // tt,),
        in_specs=[seq_spec, _const_spec(conv_state.shape), _const_spec(h0.shape)]
                 + [_const_spec(a.shape) for a in consts],
        out_specs=[seq_spec, _const_spec(conv_state.shape), _const_spec(h0.shape)],
        out_shape=[jax.ShapeDtypeStruct(xa.shape, BF16), jax.ShapeDtypeStruct(conv_state.shape, F32),
                   jax.ShapeDtypeStruct(h0.shape, F32)],
        scratch_shapes=[pltpu.VMEM((rows + halo, W), F32), pltpu.VMEM((rows, W), F32),
                        pltpu.VMEM((rows, W), F32), pltpu.VMEM((nb, W), F32)],
        compiler_params=_cparams(("arbitrary",)),
        name="rglru",
    )(xa, conv_state, h0, *consts)


def _gelu_tanh(x):
    return 0.5 * x * (1.0 + jnp.tanh(math.sqrt(2.0 / math.pi) * (x + 0.044715 * (x * x * x))))


def _s5_kernel(nb, tt, chunk, u_ref, s0r_ref, s0i_ref, ar_ref, ai_ref, bre_ref, bim_ref, cre_ref, cim_ref,
               d_ref, gw_ref, gb_ref, yb_ref, lastr_ref, lasti_ref, sre, sim, car_r, car_i):
    n_state = sre.shape[1]
    n_half = bre_ref.shape[0]
    kw = bre_ref.shape[1]
    sw = bre_ref.shape[2]
    yw = cre_ref.shape[2]

    @pl.when(pl.program_id(0) == 0)
    def _():
        car_r[...] = s0r_ref[...]
        car_i[...] = s0i_ref[...]

    u = _load_time_major(u_ref)
    ub = u.astype(BF16)
    for hf in range(n_half):
        sre[:, hf * sw:(hf + 1) * sw] = _dot(ub[:, hf * kw:(hf + 1) * kw], bre_ref[hf])
        sim[:, hf * sw:(hf + 1) * sw] = _dot(ub[:, hf * kw:(hf + 1) * kw], bim_ref[hf])

    for c0 in range(0, n_state, chunk):
        cs = slice(c0, c0 + chunk)
        ar = jnp.broadcast_to(ar_ref[:, cs], (nb, chunk))
        ai = jnp.broadcast_to(ai_ref[:, cs], (nb, chunk))

        def step(t, carry):
            s_r, s_i = carry
            rs = pl.ds(pl.multiple_of(t * nb, nb), nb)
            n_r = ar * s_r - ai * s_i + sre[rs, cs]
            n_i = ar * s_i + ai * s_r + sim[rs, cs]
            sre[rs, cs] = n_r
            sim[rs, cs] = n_i
            return n_r, n_i

        s_r, s_i = lax.fori_loop(0, tt, step, (car_r[:, cs], car_i[:, cs]), unroll=True)
        car_r[:, cs] = s_r
        car_i[:, cs] = s_i

    lastr_ref[...] = car_r[...]
    lasti_ref[...] = car_i[...]

    ys = []
    for hf in range(n_half):
        ss = slice(hf * sw, (hf + 1) * sw)
        ys.append(_dot(sre[:, ss].astype(BF16), cre_ref[hf]) - _dot(sim[:, ss].astype(BF16), cim_ref[hf]))
    y = jnp.concatenate(ys, axis=1) + d_ref[...] * u
    z = _gelu_tanh(y).astype(BF16)
    w_half = gw_ref.shape[1] // 2
    val = _dot(z, gw_ref[:, :w_half]) + gb_ref[:, :w_half]
    gate = _dot(z, gw_ref[:, w_half:]) + gb_ref[:, w_half:]
    _store_time_major(yb_ref, val * jax.nn.sigmoid(gate))


def _s5(u, s0r, s0i, p, nb, tt):
    seq_spec, T = _seq_specs(u, nb, tt)
    rows = tt * nb
    n_state = s0r.shape[1]
    chunk = max(LANES, min(n_state, 4 * LANES * 8 // nb))
    consts = [p['a_re'], p['a_im'], p['b_re'], p['b_im'], p['c_re'], p['c_im'], p['d'], p['glu_w'], p['glu_b']]
    return pl.pallas_call(
        functools.partial(_s5_kernel, nb, tt, chunk),
        grid=(T // tt,),
        in_specs=[seq_spec, _const_spec(s0r.shape), _const_spec(s0i.shape)]
                 + [_const_spec(a.shape) for a in consts],
        out_specs=[seq_spec, _const_spec(s0r.shape), _const_spec(s0i.shape)],
        out_shape=[jax.ShapeDtypeStruct(u.shape, BF16), jax.ShapeDtypeStruct(s0r.shape, F32),
                   jax.ShapeDtypeStruct(s0i.shape, F32)],
        scratch_shapes=[pltpu.VMEM((rows, n_state), F32), pltpu.VMEM((rows, n_state), F32),
                        pltpu.VMEM((nb, n_state), F32), pltpu.VMEM((nb, n_state), F32)],
        compiler_params=_cparams(("arbitrary",)),
        name="s5",
    )(u, s0r, s0i, *consts)


def _attn_prompt_kernel(tq, q_ref, kp_ref, kc_ref, vp_ref, vc_ref, o_ref,
                        acc, mst, lst, mask_t, one_hot, sbuf, pbuf):
    it = pl.program_id(2)
    qb = Q_BLOCK
    scale = HEAD_DIM ** -0.5
    lane = lax.broadcasted_iota(jnp.int32, (1, LANES), 1)
    head0 = lane < HEAD_DIM
    key = lax.broadcasted_iota(jnp.int32, (2 * qb, qb), 0)
    row = lax.broadcasted_iota(jnp.int32, (2 * qb, qb), 1)
    band_t = (key >= row) & (key <= row + qb)
    mask_t[0] = jnp.where(band_t, 0.0, NEG_INF).astype(BF16)
    mask_t[1] = jnp.where(band_t & (key >= qb), 0.0, NEG_INF).astype(BF16)
    one_hot[...] = (key % qb == row).astype(BF16)

    blocks = []
    for g, (w, d) in enumerate(DILATED_GROUPS):
        assert w // d == qb
        nblk = tq // (qb * d)
        for idx in range(d * nblk):
            qstart = idx // nblk + d * qb * (idx % nblk)
            kstart = qstart - d * qb
            stride = {} if d == 1 else {"stride": d}
            if kstart < 0:
                ks = (pl.ds(tq + kstart, qb, **stride), pl.ds(qstart, qb, **stride))
            else:
                ks = (pl.ds(kstart, 2 * qb, **stride),)
            blocks.append((g, kstart < 0, pl.ds(qstart, qb, **stride), ks))

    def key_rows(prev_ref, cur_ref, ks):
        if len(ks) == 1:
            return cur_ref[ks[0], :].astype(BF16)
        return jnp.concatenate([prev_ref[ks[0], :], cur_ref[ks[1], :]], axis=0).astype(BF16)

    n_slot = sbuf.shape[0]
    first_tile = (it == 0).astype(jnp.int32)

    def scores(j):
        _, before_tile, qs, ks = blocks[j]
        qt = q_ref[qs, :] * scale
        q2 = jnp.concatenate([jnp.where(head0, qt, 0.0), jnp.where(head0, 0.0, qt)], axis=0).astype(BF16)
        q_aug = jnp.concatenate([q2, one_hot[...]], axis=1)
        k_aug = jnp.concatenate([key_rows(kp_ref, kc_ref, ks), mask_t[first_tile] if before_tile else mask_t[0]],
                                axis=1)
        sbuf[j % n_slot] = lax.dot_general(q_aug, k_aug, (((1,), (1,)), ((), ())), preferred_element_type=F32)

    def probs(j):
        g, _, qs, _ = blocks[j]
        sl = j % n_slot
        m = jnp.max(jnp.maximum(sbuf[sl, :, 0:LANES], sbuf[sl, :, LANES:2 * LANES]), axis=-1, keepdims=True)
        pbuf[sl, :, 0:LANES] = jnp.exp(sbuf[sl, :, 0:LANES] - m).astype(BF16)
        pbuf[sl, :, LANES:2 * LANES] = jnp.exp(sbuf[sl, :, LANES:2 * LANES] - m).astype(BF16)
        mst[g, qs, :] = jnp.where(head0, m[0:qb], m[qb:2 * qb])

    def values(j):
        g, _, qs, ks = blocks[j]
        v_aug = jnp.concatenate([key_rows(vp_ref, vc_ref, ks), jnp.ones((2 * qb, LANES), BF16)], axis=1)
        pv = _dot(pbuf[j % n_slot], v_aug)
        acc[g, qs, :] = jnp.where(head0, pv[0:qb, 0:LANES], pv[qb:2 * qb, 0:LANES])
        lst[g, qs, :] = jnp.where(head0, pv[0:qb, LANES:2 * LANES], pv[qb:2 * qb, LANES:2 * LANES])

    group = 2
    n_chunk = len(blocks) // group
    for step in range(n_chunk + 2):
        @pl.when(it >= 0)
        def _(step=step):
            for c, stage in ((step - 2, values), (step - 1, probs), (step, scores)):
                if 0 <= c < n_chunk:
                    for j in range(c * group, (c + 1) * group):
                        stage(j)

    n_g = len(DILATED_GROUPS)
    m_top = functools.reduce(jnp.maximum, [mst[g] for g in range(n_g)])
    num = 0.0
    den = 0.0
    for g in range(n_g):
        wgt = jnp.exp(mst[g] - m_top)
        num = num + wgt * acc[g]
        den = den + wgt * lst[g]
    o_ref[...] = (num / den).astype(o_ref.dtype)


def _attn_prompt(q, k, v):
    B, T, W = q.shape
    tq = W_MAX
    assert T % tq == 0 and W % LANES == 0
    n_hp = W // LANES
    cur = pl.BlockSpec((None, tq, LANES), lambda b, h, i: (b, i, h))
    prev = pl.BlockSpec((None, tq, LANES), lambda b, h, i: (b, jnp.maximum(i - 1, 0), h))
    n_g = len(DILATED_GROUPS)
    n_blk, qb2 = 12, 2 * Q_BLOCK
    return pl.pallas_call(
        functools.partial(_attn_prompt_kernel, tq),
        grid=(B, n_hp, T // tq),
        in_specs=[cur, prev, cur, prev, cur],
        out_specs=cur,
        out_shape=jax.ShapeDtypeStruct((B, T, W), BF16),
        scratch_shapes=[pltpu.VMEM((n_g, tq, LANES), F32), pltpu.VMEM((n_g, tq, LANES), F32),
                        pltpu.VMEM((n_g, tq, LANES), F32),
                        pltpu.VMEM((2, qb2, Q_BLOCK), BF16), pltpu.VMEM((qb2, Q_BLOCK), BF16),
                        pltpu.VMEM((n_blk, qb2, qb2), F32), pltpu.VMEM((n_blk, qb2, qb2), BF16)],
        compiler_params=_cparams(("parallel", "parallel", "arbitrary")),
        name="attn_prompt",
    )(q, k, k, v, v)


def _attn_sample_kernel(n_new, q_ref, kc_ref, vc_ref, kn_ref, vn_ref, o_ref):
    P = kc_ref.shape[0]
    W = kc_ref.shape[1]
    pad = q_ref.shape[0]
    n_rows = ATTN_HEADS * pad
    scale = HEAD_DIM ** -0.5
    q = q_ref[...]
    col_head = lax.broadcasted_iota(jnp.int32, (pad, W), 1) // HEAD_DIM
    qbd = jnp.concatenate([jnp.where(col_head == h, q, 0.0) for h in range(ATTN_HEADS)], axis=0).astype(BF16)

    def multiplicity(dist):
        cnt = jnp.zeros(dist.shape, F32)
        for w, d in DILATED_GROUPS:
            cnt = cnt + ((dist >= 0) & (dist <= w) & (dist % d == 0)).astype(F32)
        return cnt

    nt = (((1,), (1,)), ((), ()))
    s_c = lax.dot_general(qbd, kc_ref[...].astype(BF16), nt, preferred_element_type=F32) * scale
    s_n = lax.dot_general(qbd, kn_ref[...].astype(BF16), nt, preferred_element_type=F32) * scale
    j_c = lax.broadcasted_iota(jnp.int32, (n_rows, P), 0) % pad
    i_c = lax.broadcasted_iota(jnp.int32, (n_rows, P), 1)
    mult_c = multiplicity(j_c + P - i_c)
    j_n = lax.broadcasted_iota(jnp.int32, (n_rows, pad), 0) % pad
    i_n = lax.broadcasted_iota(jnp.int32, (n_rows, pad), 1)
    mult_n = jnp.where(i_n < n_new, multiplicity(j_n - i_n), 0.0)
    s_c = jnp.where(mult_c > 0, s_c, NEG_INF)
    s_n = jnp.where(mult_n > 0, s_n, NEG_INF)
    m = jnp.maximum(jnp.max(s_c, axis=-1, keepdims=True), jnp.max(s_n, axis=-1, keepdims=True))
    p_c = mult_c * jnp.exp(s_c - m)
    p_n = mult_n * jnp.exp(s_n - m)
    l = jnp.sum(p_c, axis=-1, keepdims=True) + jnp.sum(p_n, axis=-1, keepdims=True)
    pv = (_dot(p_c.astype(BF16), vc_ref[...].astype(BF16)) + _dot(p_n.astype(BF16), vn_ref[...].astype(BF16))) / l
    out = jnp.zeros((pad, W), F32)
    for h in range(ATTN_HEADS):
        out = out + jnp.where(col_head == h, pv[h * pad:(h + 1) * pad], 0.0)
    o_ref[...] = out


def _attn_sample(q, k_new, v_new, k_cache, v_cache, n_new):
    B, pad, W = q.shape
    P = k_cache.shape[1]
    small = pl.BlockSpec((None, pad, W), lambda b: (b, 0, 0))
    big = pl.BlockSpec((None, P, W), lambda b: (b, 0, 0))
    return pl.pallas_call(
        functools.partial(_attn_sample_kernel, n_new),
        grid=(B,),
        in_specs=[small, big, big, small, small],
        out_specs=small,
        out_shape=jax.ShapeDtypeStruct((B, pad, W), F32),
        compiler_params=_cparams(("parallel",)),
        name="attn_sample",
    )(q, k_cache, v_cache, k_new, v_new)


def _mix_mlp_kernel(ff_chunk, x_ref, gpre_ref, wg_ref, ya_ref, yb_ref, yc_ref, wa_ref, wb_ref, wc_ref, wo_ref,
                    gpost_ref, g2pre_ref, w1_ref, w2_ref, g2post_ref, o_ref):
    x = x_ref[...]
    D = x.shape[1]
    h = _rms(x, gpre_ref[...]).astype(BF16)
    merged = None
    for i, (y_ref, w_ref) in enumerate(((ya_ref, wa_ref), (yb_ref, wb_ref), (yc_ref, wc_ref))):
        gate = jax.nn.sigmoid(_dot(h, wg_ref[:, i * D:(i + 1) * D]))
        term = gate * _dot(y_ref[...].astype(BF16), w_ref[...])
        merged = term if merged is None else merged + term
    x = x + _rms(_dot(merged.astype(BF16), wo_ref[...]), gpost_ref[...])

    h = _rms(x, g2pre_ref[...]).astype(BF16)
    acc = None
    for c0 in range(0, w1_ref.shape[1], ff_chunk):
        hid = jnp.square(jnp.maximum(_dot(h, w1_ref[:, c0:c0 + ff_chunk]), 0.0)).astype(BF16)
        part = _dot(hid, w2_ref[c0:c0 + ff_chunk, :])
        acc = part if acc is None else acc + part
    o_ref[...] = x + _rms(acc, g2post_ref[...])


def _mix_mlp(x, ya, yb, yc, p, rows):
    B, T, D = x.shape
    spec = lambda a: pl.BlockSpec((None, rows, a.shape[2]), lambda b, i: (b, i, 0))
    consts = [p['norm_pre_mix'], p['w_gates']]
    tail = [p['w_proj_a'], p['w_proj_b'], p['w_proj_c'], p['w_out'], p['norm_post_mix'],
            p['norm_pre_mlp'], p['mlp_w1'], p['mlp_w2'], p['norm_post_mlp']]
    return pl.pallas_call(
        functools.partial(_mix_mlp_kernel, 1024),
        grid=(B, T // rows),
        in_specs=[spec(x)] + [_resident_spec(a.shape) for a in consts] + [spec(ya), spec(yb), spec(yc)]
                 + [_resident_spec(a.shape) for a in tail],
        out_specs=spec(x),
        out_shape=jax.ShapeDtypeStruct((B, T, D), F32),
        compiler_params=_cparams(("parallel", "parallel")),
        name="mix_mlp",
    )(x, *consts, ya, yb, yc, *tail)


def _block_diag_tiles(w, per_tile):
    n, c, d = w.shape
    w = w.reshape(n // per_tile, per_tile, c, d)
    eye = jnp.eye(per_tile, dtype=w.dtype)
    return jnp.einsum('jpcd,pq->jpcqd', w, eye).reshape(n // per_tile, per_tile * c, per_tile * d)


def _layer_params(l, a):
    D = a['w_in'].shape[1]
    lru_w = a['conv_w'].shape[2]
    n_blk, blk = a['lru_wr'].shape[1], a['lru_wr'].shape[2]
    G, N, GC = a['ssm_b_re'].shape[1:]
    ssm_w = G * GC
    attn_w = ATTN_HEADS * HEAD_DIM
    n_front = lru_w + ssm_w + 3 * attn_w
    row = lambda v: v.reshape(1, -1).astype(F32)
    p = {
        'norm_pre_mix': row(a['norm_pre_mix'][l]), 'norm_post_mix': row(a['norm_post_mix'][l]),
        'norm_pre_mlp': row(a['norm_pre_mlp'][l]), 'norm_post_mlp': row(a['norm_post_mlp'][l]),
        'w_front': a['w_in_bf16'][l][:, :n_front], 'w_gates': a['w_in_bf16'][l][:, n_front:],
        'w_proj_a': a['w_proj_a'][l].astype(BF16), 'w_proj_b': a['w_proj_b'][l].astype(BF16),
        'w_proj_c': a['w_proj_c'][l].astype(BF16), 'w_out': a['w_out'][l].astype(BF16),
        'mlp_w1': a['mlp_w1'][l].astype(BF16), 'mlp_w2': a['mlp_w2'][l].astype(BF16),
        'widths': (lru_w, ssm_w, attn_w, attn_w, attn_w),
    }
    per_tile = MXU_TILE // blk
    p['lru'] = {
        'conv_w': a['conv_w'][l].astype(F32), 'conv_b': row(a['conv_b'][l]),
        'wr': _block_diag_tiles(a['lru_wr'][l], per_tile).astype(BF16), 'br': row(a['lru_br'][l]),
        'wi': _block_diag_tiles(a['lru_wi'][l], per_tile).astype(BF16), 'bi': row(a['lru_bi'][l]),
        'lam': row(a['lru_lambda'][l]),
    }
    lr, li = a['ssm_lambda_re'][l].astype(F32), a['ssm_lambda_im'][l].astype(F32)
    dt = jnp.exp(a['ssm_log_dt'][l].astype(F32))[:, None]
    mag = jnp.exp(lr * dt)
    ang = li * dt
    ab_re, ab_im = mag * jnp.cos(ang), mag * jnp.sin(ang)
    den = lr * lr + li * li
    f_re = ((ab_re - 1.0) * lr + ab_im * li) / den
    f_im = (ab_im * lr - (ab_re - 1.0) * li) / den
    b_re, b_im = a['ssm_b_re'][l].astype(F32), a['ssm_b_im'][l].astype(F32)
    bb_re = f_re[..., None] * b_re - f_im[..., None] * b_im
    bb_im = f_re[..., None] * b_im + f_im[..., None] * b_re
    g_half = G // 2
    to_b = lambda m: _block_diag_tiles(jnp.swapaxes(m, 1, 2), g_half).astype(BF16)
    to_c = lambda m: _block_diag_tiles(jnp.swapaxes(m, 1, 2), g_half).astype(BF16)
    p['s5'] = {
        'a_re': ab_re.reshape(1, G * N), 'a_im': ab_im.reshape(1, G * N),
        'b_re': to_b(bb_re), 'b_im': to_b(bb_im),
        'c_re': to_c(a['ssm_c_re'][l].astype(F32)), 'c_im': to_c(a['ssm_c_im'][l].astype(F32)),
        'd': row(a['ssm_d'][l]), 'glu_w': a['glu_w'][l].astype(BF16), 'glu_b': row(a['glu_b'][l]),
    }
    return p


def _tm(state):
    nb, k, w = state.shape
    return jnp.swapaxes(state, 0, 1).reshape(k * nb, w)


def _layer_prompt(x, p, rows, tt, keep, layer, depth, tails):
    B, T, D = x.shape
    lru_w, ssm_w, attn_w = p['widths'][0], p['widths'][1], p['widths'][2]
    xa, u, q, k, v, k_tail, v_tail = _in_proj(x, p['norm_pre_mix'], p['w_front'], p['widths'], rows,
                                              keep, layer, depth, tails)
    n_state = p['s5']['a_re'].shape[1]
    halo = (p['lru']['conv_w'].shape[0] - 1) * B
    ya, nconv, hlast = _rglru(xa, jnp.zeros((halo, lru_w), F32), jnp.zeros((B, lru_w), F32), p['lru'], B, tt)
    yb, s_re, s_im = _s5(u, jnp.zeros((B, n_state), F32), jnp.zeros((B, n_state), F32), p['s5'], B, tt)
    yc = _attn_prompt(q, k, v)
    x = _mix_mlp(x, ya, yb, yc, p, rows)
    return x, nconv, hlast, s_re, s_im, k_tail, v_tail


def _layer_sample(x, conv_state, h0, s0r, s0i, k_cache, v_cache, p, nb, n_new):
    rows = x.shape[1]
    lru_w, ssm_w, attn_w = p['widths'][0], p['widths'][1], p['widths'][2]
    xa, u, q, k, v = _in_proj(x, p['norm_pre_mix'], p['w_front'], p['widths'], rows)
    ya, nconv, hlast = _rglru(xa[0], conv_state, h0, p['lru'], nb, n_new)
    yb, s_re, s_im = _s5(u[0], s0r, s0i, p['s5'], nb, n_new)
    pad = 8

    def bm(arr):
        arr = jnp.swapaxes(arr.reshape(n_new, nb, attn_w), 0, 1)
        return jnp.pad(arr, ((0, 0), (0, pad - n_new), (0, 0)))

    yc = _attn_sample(bm(q), bm(k), bm(v), k_cache, v_cache, n_new)[:, :n_new]
    yc = jnp.swapaxes(yc, 0, 1).reshape(1, rows, attn_w)
    x = _mix_mlp(x, ya[None], yb[None], yc, p, rows)
    return x, nconv, hlast, s_re, s_im, k, v


def kernel(x_prompt, x_sample, state_conv, state_lru, state_ssm_re, state_ssm_im, cache_k, cache_v, norm_pre_mix, norm_post_mix, norm_pre_mlp, norm_post_mlp, w_in, conv_w, conv_b, lru_wr, lru_br, lru_wi, lru_bi, lru_lambda, ssm_lambda_re, ssm_lambda_im, ssm_log_dt, ssm_b_re, ssm_b_im, ssm_c_re, ssm_c_im, ssm_d, glu_w, glu_b, w_proj_a, w_proj_b, w_proj_c, w_out, mlp_w1, mlp_w2):
    a = dict(norm_pre_mix=norm_pre_mix, norm_post_mix=norm_post_mix, norm_pre_mlp=norm_pre_mlp,
             norm_post_mlp=norm_post_mlp, w_in=w_in, conv_w=conv_w, conv_b=conv_b, lru_wr=lru_wr, lru_br=lru_br,
             lru_wi=lru_wi, lru_bi=lru_bi, lru_lambda=lru_lambda, ssm_lambda_re=ssm_lambda_re,
             ssm_lambda_im=ssm_lambda_im, ssm_log_dt=ssm_log_dt, ssm_b_re=ssm_b_re, ssm_b_im=ssm_b_im,
             ssm_c_re=ssm_c_re, ssm_c_im=ssm_c_im, ssm_d=ssm_d, glu_w=glu_w, glu_b=glu_b, w_proj_a=w_proj_a,
             w_proj_b=w_proj_b, w_proj_c=w_proj_c, w_out=w_out, mlp_w1=mlp_w1, mlp_w2=mlp_w2,
             w_in_bf16=w_in.astype(BF16))
    depth = w_in.shape[0]
    Bp, Tp, D = x_prompt.shape
    Bs, Ts, _ = x_sample.shape
    G, N = state_ssm_re.shape[2:]
    attn_w = ATTN_HEADS * HEAD_DIM
    P = cache_k.shape[2]
    assert P == W_MAX, "the window buffer must hold exactly the widest window"
    keep = min(W_MAX, Tp)
    conv_k = state_conv.shape[2]

    xp = x_prompt
    xs = jnp.swapaxes(x_sample, 0, 1).reshape(1, Ts * Bs, D)
    outs = [[] for _ in range(12)]
    tails = ()
    for l in range(depth):
        p = _layer_params(l, a)
        xp, c_, h_, r_, i_, pk, pv = _layer_prompt(xp, p, rows=512, tt=64, keep=keep, layer=l, depth=depth,
                                                   tails=tails)
        tails = (pk, pv)
        for lst, val in zip(outs[:4], (jnp.swapaxes(c_.reshape(conv_k, Bp, -1), 0, 1), h_,
                                       r_.reshape(Bp, G, N), i_.reshape(Bp, G, N))):
            lst.append(val)
        xs, c_, h_, r_, i_, k_, v_ = _layer_sample(
            xs, _tm(state_conv[l]), state_lru[l], state_ssm_re[l].reshape(Bs, G * N),
            state_ssm_im[l].reshape(Bs, G * N), cache_k[l].reshape(Bs, P, attn_w),
            cache_v[l].reshape(Bs, P, attn_w), p, Bs, Ts)
        sk = jnp.swapaxes(k_.reshape(Ts, Bs, ATTN_HEADS, HEAD_DIM), 0, 1)
        sv = jnp.swapaxes(v_.reshape(Ts, Bs, ATTN_HEADS, HEAD_DIM), 0, 1)
        for lst, val in zip(outs[6:], (jnp.swapaxes(c_.reshape(conv_k, Bs, -1), 0, 1), h_,
                                       r_.reshape(Bs, G, N), i_.reshape(Bs, G, N), sk, sv)):
            lst.append(val)
    ys = jnp.swapaxes(xs.reshape(Ts, Bs, D), 0, 1)
    stacked = [jnp.stack(o) if o else None for o in outs]
    stacked[4], stacked[5] = tails
    return (xp, ys) + tuple(stacked)
```

```python
import functools
import math

import jax
import jax.numpy as jnp
from jax import lax
from jax.experimental import pallas as pl
from jax.experimental.pallas import tpu as pltpu

F32 = jnp.float32
BF16 = jnp.bfloat16

RMS_EPS = 1e-6
NEG_INF = -1e30
RG_C = 8.0
ATTN_HEADS = 12
HEAD_DIM = 64
DILATED_GROUPS = ((128, 1), (512, 4), (2048, 16))
W_MAX = max(w for w, _ in DILATED_GROUPS)
Q_BLOCK = 128

LANES = 128
MXU_TILE = 256
VMEM_LIMIT = 56 * 1024 * 1024


def _cparams(sem):
    return pltpu.CompilerParams(dimension_semantics=sem, vmem_limit_bytes=VMEM_LIMIT)


def _rms(x, g):
    return x * lax.rsqrt(jnp.mean(x * x, axis=-1, keepdims=True) + RMS_EPS) * g


def _dot(a, b):
    return jnp.dot(a, b, preferred_element_type=F32)


def _neg_expm1_2x(x, ex):
    e2 = ex * ex
    return jnp.where(jnp.abs(x) > 0.25, 1.0 - e2, -jnp.tanh(x) * (e2 + 1.0))


def _sigmoid(x):
    return 0.5 * jnp.tanh(0.5 * x) + 0.5


def _const_spec(shape):
    nd = len(shape)
    return pl.BlockSpec(shape, lambda *_: (0,) * nd)


def _resident_spec(shape):
    nd = len(shape)
    return pl.BlockSpec(shape, lambda *_: (0,) * nd, pipeline_mode=pl.Buffered(1))


def _in_proj_kernel(splits, tail_from, n_alias, x_ref, g_ref, w_ref, *refs):
    n = len(splits)
    out_refs = refs[n_alias:]
    h = _rms(x_ref[...], g_ref[...]).astype(BF16)
    for (lo, hi), o_ref in zip(splits, out_refs[:n]):
        o_ref[...] = _dot(h, w_ref[:, lo:hi])
    if len(out_refs) > n:
        @pl.when(pl.program_id(1) >= tail_from)
        def _():
            for src, dst in zip(out_refs[n - 2:n], out_refs[n:]):
                dst[...] = src[...].reshape(dst.shape)


def _in_proj(x, g, w, widths, rows, keep=0, layer=0, depth=1, tails=()):
    B, T, D = x.shape
    splits, lo = [], 0
    for wd in widths:
        splits.append((lo, lo + wd))
        lo += wd
    in_specs = [pl.BlockSpec((None, rows, D), lambda b, i: (b, i, 0)), _const_spec((1, D)), _const_spec(w.shape)]
    out_specs = [pl.BlockSpec((None, rows, wd), lambda b, i: (b, i, 0)) for wd in widths]
    out_shape = [jax.ShapeDtypeStruct((B, T, wd), F32) for wd in widths]
    tail_from = (T - keep) // rows
    aliases = {}
    if keep:
        assert keep % rows == 0 and widths[-1] == widths[-2] == ATTN_HEADS * HEAD_DIM
        tail = pl.BlockSpec((None, None, rows, ATTN_HEADS, HEAD_DIM),
                            lambda b, i: (layer, b, jnp.maximum(i - tail_from, 0), 0, 0))
        out_specs += [tail, tail]
        out_shape += [jax.ShapeDtypeStruct((depth, B, keep, ATTN_HEADS, HEAD_DIM), F32)] * 2
        in_specs += [pl.BlockSpec(memory_space=pl.ANY)] * len(tails)
        aliases = {len(in_specs) - len(tails) + j: len(widths) + j for j in range(len(tails))}
    return pl.pallas_call(
        functools.partial(_in_proj_kernel, tuple(splits), tail_from, len(tails)),
        grid=(B, T // rows),
        in_specs=in_specs,
        out_specs=out_specs,
        out_shape=out_shape,
        input_output_aliases=aliases,
        compiler_params=_cparams(("parallel", "arbitrary")),
        name="in_proj",
    )(x, g.reshape(1, D), w, *tails)


def _load_time_major(ref):
    if len(ref.shape) == 2:
        return ref[...]
    nb, tt, w = ref.shape
    return jnp.swapaxes(ref[...], 0, 1).reshape(tt * nb, w)


def _store_time_major(ref, val):
    if len(ref.shape) == 2:
        ref[...] = val.astype(ref.dtype)
    else:
        nb, tt, w = ref.shape
        ref[...] = jnp.swapaxes(val.reshape(tt, nb, w), 0, 1).astype(ref.dtype)


def _seq_specs(x, nb, tt):
    w = x.shape[-1]
    if x.ndim == 3:
        return pl.BlockSpec((nb, tt, w), lambda i: (0, i, 0)), x.shape[1]
    return pl.BlockSpec((tt * nb, w), lambda i: (i, 0)), x.shape[0] // nb


def _rglru_kernel(nb, tt, xa_ref, cs_ref, h0_ref, cw_ref, cb_ref, wr_ref, br_ref, wi_ref, bi_ref, lam_ref,
                  ya_ref, nconv_ref, hlast_ref, xe_buf, a_buf, b_buf, h_carry):
    rows = tt * nb
    halo = cs_ref.shape[0]
    n_tap = cw_ref.shape[0]

    @pl.when(pl.program_id(0) == 0)
    def _():
        xe_buf[0:halo] = cs_ref[...]
        h_carry[...] = h0_ref[...]

    xe_buf[halo:halo + rows] = _load_time_major(xa_ref)
    xc = cb_ref[...] + sum(cw_ref[i:i + 1, :] * xe_buf[i * nb:i * nb + rows] for i in range(n_tap))
    tail = xe_buf[rows:rows + halo]
    nconv_ref[...] = tail
    xe_buf[0:halo] = tail

    xcb = xc.astype(BF16)
    c = -RG_C * jax.nn.softplus(-lam_ref[...])
    for j in range(wr_ref.shape[0]):
        sl = slice(j * MXU_TILE, (j + 1) * MXU_TILE)
        r = _sigmoid(_dot(xcb[:, sl], wr_ref[j]) + br_ref[:, sl])
        ig = _sigmoid(_dot(xcb[:, sl], wi_ref[j]) + bi_ref[:, sl])
        log_a = c[:, sl] * r
        a = jnp.exp(log_a)
        a_buf[:, sl] = a
        mult = jnp.sqrt(jnp.maximum(_neg_expm1_2x(log_a, a), 0.0))
        b_buf[:, sl] = mult * (ig * xc[:, sl])

    def step(t, h):
        rs = pl.ds(pl.multiple_of(t * nb, nb), nb)
        h = a_buf[rs, :] * h + b_buf[rs, :]
        a_buf[rs, :] = h
        return h

    h = lax.fori_loop(0, tt, step, h_carry[...], unroll=True)
    h_carry[...] = h
    hlast_ref[...] = h
    _store_time_major(ya_ref, a_buf[...])


def _rglru(xa, conv_state, h0, p, nb, tt):
    W = xa.shape[-1]
    seq_spec, T = _seq_specs(xa, nb, tt)
    rows = tt * nb
    halo = conv_state.shape[0]
    assert rows >= halo and T % tt == 0
    consts = [p['conv_w'], p['conv_b'], p['wr'], p['br'], p['wi'], p['bi'], p['lam']]
    return pl.pallas_call(
        functools.partial(_rglru_kernel, nb, tt),
        grid=(T // tt,),
        in_specs=[seq_spec, _const_spec(conv_state.shape), _const_spec(h0.shape)]
                 + [_const_spec(a.shape) for a in consts],
        out_specs=[seq_spec, _const_spec(conv_state.shape), _const_spec(h0.shape)],
        out_shape=[jax.ShapeDtypeStruct(xa.shape, BF16), jax.ShapeDtypeStruct(conv_state.shape, F32),
                   jax.ShapeDtypeStruct(h0.shape, F32)],
        scratch_shapes=[pltpu.VMEM((rows + halo, W), F32), pltpu.VMEM((rows, W), F32),
                        pltpu.VMEM((rows, W), F32), pltpu.VMEM((nb, W), F32)],
        compiler_params=_cparams(("arbitrary",)),
        name="rglru",
    )(xa, conv_state, h0, *consts)


def _gelu_tanh(x):
    return 0.5 * x * (1.0 + jnp.tanh(math.sqrt(2.0 / math.pi) * (x + 0.044715 * (x * x * x))))


def _s5_kernel(nb, tt, chunk, u_ref, s0r_ref, s0i_ref, ar_ref, ai_ref, bre_ref, bim_ref, cre_ref, cim_ref,
               d_ref, gw_ref, gb_ref, yb_ref, lastr_ref, lasti_ref, sre, sim, car_r, car_i):
    n_state = sre.shape[1]
    n_half = bre_ref.shape[0]
    kw = bre_ref.shape[1]
    sw = bre_ref.shape[2]
    yw = cre_ref.shape[2]

    @pl.when(pl.program_id(0) == 0)
    def _():
        car_r[...] = s0r_ref[...]
        car_i[...] = s0i_ref[...]

    u = _load_time_major(u_ref)
    ub = u.astype(BF16)
    for hf in range(n_half):
        sre[:, hf * sw:(hf + 1) * sw] = _dot(ub[:, hf * kw:(hf + 1) * kw], bre_ref[hf])
        sim[:, hf * sw:(hf + 1) * sw] = _dot(ub[:, hf * kw:(hf + 1) * kw], bim_ref[hf])

    for c0 in range(0, n_state, chunk):
        cs = slice(c0, c0 + chunk)
        ar = jnp.broadcast_to(ar_ref[:, cs], (nb, chunk))
        ai = jnp.broadcast_to(ai_ref[:, cs], (nb, chunk))

        def step(t, carry):
            s_r, s_i = carry
            rs = pl.ds(pl.multiple_of(t * nb, nb), nb)
            n_r = ar * s_r - ai * s_i + sre[rs, cs]
            n_i = ar * s_i + ai * s_r + sim[rs, cs]
            sre[rs, cs] = n_r
            sim[rs, cs] = n_i
            return n_r, n_i

        s_r, s_i = lax.fori_loop(0, tt, step, (car_r[:, cs], car_i[:, cs]), unroll=True)
        car_r[:, cs] = s_r
        car_i[:, cs] = s_i

    lastr_ref[...] = car_r[...]
    lasti_ref[...] = car_i[...]

    ys = []
    for hf in range(n_half):
        ss = slice(hf * sw, (hf + 1) * sw)
        ys.append(_dot(sre[:, ss].astype(BF16), cre_ref[hf]) - _dot(sim[:, ss].astype(BF16), cim_ref[hf]))
    y = jnp.concatenate(ys, axis=1) + d_ref[...] * u
    z = _gelu_tanh(y).astype(BF16)
    w_half = gw_ref.shape[1] // 2
    val = _dot(z, gw_ref[:, :w_half]) + gb_ref[:, :w_half]
    gate = _dot(z, gw_ref[:, w_half:]) + gb_ref[:, w_half:]
    _store_time_major(yb_ref, val * jax.nn.sigmoid(gate))


def _s5(u, s0r, s0i, p, nb, tt):
    seq_spec, T = _seq_specs(u, nb, tt)
    rows = tt * nb
    n_state = s0r.shape[1]
    chunk = max(LANES, min(n_state, 4 * LANES * 8 // nb))
    consts = [p['a_re'], p['a_im'], p['b_re'], p['b_im'], p['c_re'], p['c_im'], p['d'], p['glu_w'], p['glu_b']]
    return pl.pallas_call(
        functools.partial(_s5_kernel, nb, tt, chunk),
        grid=(T // tt,),
        in_specs=[seq_spec, _const_spec(s0r.shape), _const_spec(s0i.shape)]
                 + [_const_spec(a.shape) for a in consts],
        out_specs=[seq_spec, _const_spec(s0r.shape), _const_spec(s0i.shape)],
        out_shape=[jax.ShapeDtypeStruct(u.shape, BF16), jax.ShapeDtypeStruct(s0r.shape, F32),
                   jax.ShapeDtypeStruct(s0i.shape, F32)],
        scratch_shapes=[pltpu.VMEM((rows, n_state), F32), pltpu.VMEM((rows, n_state), F32),
                        pltpu.VMEM((nb, n_state), F32), pltpu.VMEM((nb, n_state), F32)],
        compiler_params=_cparams(("arbitrary",)),
        name="s5",
    )(u, s0r, s0i, *consts)


def _attn_prompt_kernel(tq, q_ref, kp_ref, kc_ref, vp_ref, vc_ref, o_ref,
                        acc, mst, lst, mask_t, one_hot, sbuf, pbuf):
    it = pl.program_id(2)
    qb = Q_BLOCK
    scale = HEAD_DIM ** -0.5
    lane = lax.broadcasted_iota(jnp.int32, (1, LANES), 1)
    head0 = lane < HEAD_DIM
    key = lax.broadcasted_iota(jnp.int32, (2 * qb, qb), 0)
    row = lax.broadcasted_iota(jnp.int32, (2 * qb, qb), 1)
    band_t = (key >= row) & (key <= row + qb)
    mask_t[0] = jnp.where(band_t, 0.0, NEG_INF).astype(BF16)
    mask_t[1] = jnp.where(band_t & (key >= qb), 0.0, NEG_INF).astype(BF16)
    one_hot[...] = (key % qb == row).astype(BF16)

    blocks = []
    for g, (w, d) in enumerate(DILATED_GROUPS):
        assert w // d == qb
        nblk = tq // (qb * d)
        for idx in range(d * nblk):
            qstart = idx // nblk + d * qb * (idx % nblk)
            kstart = qstart - d * qb
            stride = {} if d == 1 else {"stride": d}
            if kstart < 0:
                ks = (pl.ds(tq + kstart, qb, **stride), pl.ds(qstart, qb, **stride))
            else:
                ks = (pl.ds(kstart, 2 * qb, **stride),)
            blocks.append((g, kstart < 0, pl.ds(qstart, qb, **stride), ks))

    def key_rows(prev_ref, cur_ref, ks):
        if len(ks) == 1:
            return cur_ref[ks[0], :].astype(BF16)
        return jnp.concatenate([prev_ref[ks[0], :], cur_ref[ks[1], :]], axis=0).astype(BF16)

    n_slot = sbuf.shape[0]
    first_tile = (it == 0).astype(jnp.int32)

    def scores(j):
        _, before_tile, qs, ks = blocks[j]
        qt = q_ref[qs, :] * scale
        q2 = jnp.concatenate([jnp.where(head0, qt, 0.0), jnp.where(head0, 0.0, qt)], axis=0).astype(BF16)
        q_aug = jnp.concatenate([q2, one_hot[...]], axis=1)
        k_aug = jnp.concatenate([key_rows(kp_ref, kc_ref, ks), mask_t[first_tile] if before_tile else mask_t[0]],
                                axis=1)
        sbuf[j % n_slot] = lax.dot_general(q_aug, k_aug, (((1,), (1,)), ((), ())), preferred_element_type=F32)

    def probs(j):
        g, _, qs, _ = blocks[j]
        sl = j % n_slot
        m = jnp.max(jnp.maximum(sbuf[sl, :, 0:LANES], sbuf[sl, :, LANES:2 * LANES]), axis=-1, keepdims=True)
        pbuf[sl, :, 0:LANES] = jnp.exp(sbuf[sl, :, 0:LANES] - m).astype(BF16)
        pbuf[sl, :, LANES:2 * LANES] = jnp.exp(sbuf[sl, :, LANES:2 * LANES] - m).astype(BF16)
        mst[g, qs, :] = jnp.where(head0, m[0:qb], m[qb:2 * qb])

    def values(j):
        g, _, qs, ks = blocks[j]
        v_aug = jnp.concatenate([key_rows(vp_ref, vc_ref, ks), jnp.ones((2 * qb, LANES), BF16)], axis=1)
        pv = _dot(pbuf[j % n_slot], v_aug)
        acc[g, qs, :] = jnp.where(head0, pv[0:qb, 0:LANES], pv[qb:2 * qb, 0:LANES])
        lst[g, qs, :] = jnp.where(head0, pv[0:qb, LANES:2 * LANES], pv[qb:2 * qb, LANES:2 * LANES])

    group = 2
    n_chunk = len(blocks) // group
    for step in range(n_chunk + 2):
        @pl.when(it >= 0)
        def _(step=step):
            for c, stage in ((step - 2, values), (step - 1, probs), (step, scores)):
                if 0 <= c < n_chunk:
                    for j in range(c * group, (c + 1) * group):
                        stage(j)

    n_g = len(DILATED_GROUPS)
    m_top = functools.reduce(jnp.maximum, [mst[g] for g in range(n_g)])
    num = 0.0
    den = 0.0
    for g in range(n_g):
        wgt = jnp.exp(mst[g] - m_top)
        num = num + wgt * acc[g]
        den = den + wgt * lst[g]
    o_ref[...] = (num / den).astype(o_ref.dtype)


def _attn_prompt(q, k, v):
    B, T, W = q.shape
    tq = W_MAX
    assert T % tq == 0 and W % LANES == 0
    n_hp = W // LANES
    cur = pl.BlockSpec((None, tq, LANES), lambda b, h, i: (b, i, h))
    prev = pl.BlockSpec((None, tq, LANES), lambda b, h, i: (b, jnp.maximum(i - 1, 0), h))
    n_g = len(DILATED_GROUPS)
    n_blk, qb2 = 12, 2 * Q_BLOCK
    return pl.pallas_call(
        functools.partial(_attn_prompt_kernel, tq),
        grid=(B, n_hp, T // tq),
        in_specs=[cur, prev, cur, prev, cur],
        out_specs=cur,
        out_shape=jax.ShapeDtypeStruct((B, T, W), BF16),
        scratch_shapes=[pltpu.VMEM((n_g, tq, LANES), F32), pltpu.VMEM((n_g, tq, LANES), F32),
                        pltpu.VMEM((n_g, tq, LANES), F32),
                        pltpu.VMEM((2, qb2, Q_BLOCK), BF16), pltpu.VMEM((qb2, Q_BLOCK), BF16),
                        pltpu.VMEM((n_blk, qb2, qb2), F32), pltpu.VMEM((n_blk, qb2, qb2), BF16)],
        compiler_params=_cparams(("parallel", "parallel", "arbitrary")),
        name="attn_prompt",
    )(q, k, k, v, v)


def _attn_sample_kernel(n_new, q_ref, kc_ref, vc_ref, kn_ref, vn_ref, o_ref):
    P = kc_ref.shape[0]
    W = kc_ref.shape[1]
    pad = q_ref.shape[0]
    n_rows = ATTN_HEADS * pad
    scale = HEAD_DIM ** -0.5
    q = q_ref[...]
    col_head = lax.broadcasted_iota(jnp.int32, (pad, W), 1) // HEAD_DIM
    qbd = jnp.concatenate([jnp.where(col_head == h, q, 0.0) for h in range(ATTN_HEADS)], axis=0).astype(BF16)

    def multiplicity(dist):
        cnt = jnp.zeros(dist.shape, F32)
        for w, d in DILATED_GROUPS:
            cnt = cnt + ((dist >= 0) & (dist <= w) & (dist % d == 0)).astype(F32)
        return cnt

    nt = (((1,), (1,)), ((), ()))
    s_c = lax.dot_general(qbd, kc_ref[...].astype(BF16), nt, preferred_element_type=F32) * scale
    s_n = lax.dot_general(qbd, kn_ref[...].astype(BF16), nt, preferred_element_type=F32) * scale
    j_c = lax.broadcasted_iota(jnp.int32, (n_rows, P), 0) % pad
    i_c = lax.broadcasted_iota(jnp.int32, (n_rows, P), 1)
    mult_c = multiplicity(j_c + P - i_c)
    j_n = lax.broadcasted_iota(jnp.int32, (n_rows, pad), 0) % pad
    i_n = lax.broadcasted_iota(jnp.int32, (n_rows, pad), 1)
    mult_n = jnp.where(i_n < n_new, multiplicity(j_n - i_n), 0.0)
    s_c = jnp.where(mult_c > 0, s_c, NEG_INF)
    s_n = jnp.where(mult_n > 0, s_n, NEG_INF)
    m = jnp.maximum(jnp.max(s_c, axis=-1, keepdims=True), jnp.max(s_n, axis=-1, keepdims=True))
    p_c = mult_c * jnp.exp(s_c - m)
    p_n = mult_n * jnp.exp(s_n - m)
    l = jnp.sum(p_c, axis=-1, keepdims=True) + jnp.sum(p_n, axis=-1, keepdims=True)
    pv = (_dot(p_c.astype(BF16), vc_ref[...].astype(BF16)) + _dot(p_n.astype(BF16), vn_ref[...].astype(BF16))) / l
    out = jnp.zeros((pad, W), F32)
    for h in range(ATTN_HEADS):
        out = out + jnp.where(col_head == h, pv[h * pad:(h + 1) * pad], 0.0)
    o_ref[...] = out


def _attn_sample(q, k_new, v_new, k_cache, v_cache, layer, n_new):
    B, pad, W = q.shape
    P = k_cache.shape[2]
    small = pl.BlockSpec((None, pad, W), lambda b: (b, 0, 0))
    big = pl.BlockSpec((None, None, P, W), lambda b: (layer, b, 0, 0))
    return pl.pallas_call(
        functools.partial(_attn_sample_kernel, n_new),
        grid=(B,),
        in_specs=[small, big, big, small, small],
        out_specs=small,
        out_shape=jax.ShapeDtypeStruct((B, pad, W), F32),
        compiler_params=_cparams(("parallel",)),
        name="attn_sample",
    )(q, k_cache, v_cache, k_new, v_new)


def _mix_mlp_kernel(ff_chunk, x_ref, gpre_ref, wg_ref, ya_ref, yb_ref, yc_ref, wa_ref, wb_ref, wc_ref, wo_ref,
                    gpost_ref, g2pre_ref, w1_ref, w2_ref, g2post_ref, o_ref):
    x = x_ref[...]
    D = x.shape[1]
    h = _rms(x, gpre_ref[...]).astype(BF16)
    merged = None
    for i, (y_ref, w_ref) in enumerate(((ya_ref, wa_ref), (yb_ref, wb_ref), (yc_ref, wc_ref))):
        gate = jax.nn.sigmoid(_dot(h, wg_ref[:, i * D:(i + 1) * D]))
        term = gate * _dot(y_ref[...].astype(BF16), w_ref[...])
        merged = term if merged is None else merged + term
    x = x + _rms(_dot(merged.astype(BF16), wo_ref[...]), gpost_ref[...])

    h = _rms(x, g2pre_ref[...]).astype(BF16)
    acc = None
    for c0 in range(0, w1_ref.shape[1], ff_chunk):
        hid = jnp.square(jnp.maximum(_dot(h, w1_ref[:, c0:c0 + ff_chunk]), 0.0)).astype(BF16)
        part = _dot(hid, w2_ref[c0:c0 + ff_chunk, :])
        acc = part if acc is None else acc + part
    o_ref[...] = x + _rms(acc, g2post_ref[...])


def _mix_mlp(x, ya, yb, yc, p, rows):
    B, T, D = x.shape
    spec = lambda a: pl.BlockSpec((None, rows, a.shape[2]), lambda b, i: (b, i, 0))
    consts = [p['norm_pre_mix'], p['w_gates']]
    tail = [p['w_proj_a'], p['w_proj_b'], p['w_proj_c'], p['w_out'], p['norm_post_mix'],
            p['norm_pre_mlp'], p['mlp_w1'], p['mlp_w2'], p['norm_post_mlp']]
    return pl.pallas_call(
        functools.partial(_mix_mlp_kernel, 1024),
        grid=(B, T // rows),
        in_specs=[spec(x)] + [_resident_spec(a.shape) for a in consts] + [spec(ya), spec(yb), spec(yc)]
                 + [_resident_spec(a.shape) for a in tail],
        out_specs=spec(x),
        out_shape=jax.ShapeDtypeStruct((B, T, D), F32),
        compiler_params=_cparams(("parallel", "parallel")),
        name="mix_mlp",
    )(x, *consts, ya, yb, yc, *tail)


def _block_diag_tiles(w, per_tile):
    n, c, d = w.shape
    w = w.reshape(n // per_tile, per_tile, c, d)
    eye = jnp.eye(per_tile, dtype=w.dtype)
    return jnp.einsum('jpcd,pq->jpcqd', w, eye).reshape(n // per_tile, per_tile * c, per_tile * d)


def _layer_params(l, a):
    D = a['w_in'].shape[1]
    lru_w = a['conv_w'].shape[2]
    n_blk, blk = a['lru_wr'].shape[1], a['lru_wr'].shape[2]
    G, N, GC = a['ssm_b_re'].shape[1:]
    ssm_w = G * GC
    attn_w = ATTN_HEADS * HEAD_DIM
    n_front = lru_w + ssm_w + 3 * attn_w
    row = lambda v: v.reshape(1, -1).astype(F32)
    p = {
        'norm_pre_mix': row(a['norm_pre_mix'][l]), 'norm_post_mix': row(a['norm_post_mix'][l]),
        'norm_pre_mlp': row(a['norm_pre_mlp'][l]), 'norm_post_mlp': row(a['norm_post_mlp'][l]),
        'w_front': a['w_in_bf16'][l][:, :n_front], 'w_gates': a['w_in_bf16'][l][:, n_front:],
        'w_proj_a': a['w_proj_a'][l].astype(BF16), 'w_proj_b': a['w_proj_b'][l].astype(BF16),
        'w_proj_c': a['w_proj_c'][l].astype(BF16), 'w_out': a['w_out'][l].astype(BF16),
        'mlp_w1': a['mlp_w1'][l].astype(BF16), 'mlp_w2': a['mlp_w2'][l].astype(BF16),
        'widths': (lru_w, ssm_w, attn_w, attn_w, attn_w),
    }
    per_tile = MXU_TILE // blk
    p['lru'] = {
        'conv_w': a['conv_w'][l].astype(F32), 'conv_b': row(a['conv_b'][l]),
        'wr': _block_diag_tiles(a['lru_wr'][l], per_tile).astype(BF16), 'br': row(a['lru_br'][l]),
        'wi': _block_diag_tiles(a['lru_wi'][l], per_tile).astype(BF16), 'bi': row(a['lru_bi'][l]),
        'lam': row(a['lru_lambda'][l]),
    }
    lr, li = a['ssm_lambda_re'][l].astype(F32), a['ssm_lambda_im'][l].astype(F32)
    dt = jnp.exp(a['ssm_log_dt'][l].astype(F32))[:, None]
    mag = jnp.exp(lr * dt)
    ang = li * dt
    ab_re, ab_im = mag * jnp.cos(ang), mag * jnp.sin(ang)
    den = lr * lr + li * li
    f_re = ((ab_re - 1.0) * lr + ab_im * li) / den
    f_im = (ab_im * lr - (ab_re - 1.0) * li) / den
    b_re, b_im = a['ssm_b_re'][l].astype(F32), a['ssm_b_im'][l].astype(F32)
    bb_re = f_re[..., None] * b_re - f_im[..., None] * b_im
    bb_im = f_re[..., None] * b_im + f_im[..., None] * b_re
    g_half = G // 2
    to_b = lambda m: _block_diag_tiles(jnp.swapaxes(m, 1, 2), g_half).astype(BF16)
    to_c = lambda m: _block_diag_tiles(jnp.swapaxes(m, 1, 2), g_half).astype(BF16)
    p['s5'] = {
        'a_re': ab_re.reshape(1, G * N), 'a_im': ab_im.reshape(1, G * N),
        'b_re': to_b(bb_re), 'b_im': to_b(bb_im),
        'c_re': to_c(a['ssm_c_re'][l].astype(F32)), 'c_im': to_c(a['ssm_c_im'][l].astype(F32)),
        'd': row(a['ssm_d'][l]), 'glu_w': a['glu_w'][l].astype(BF16), 'glu_b': row(a['glu_b'][l]),
    }
    return p


def _tm(state):
    nb, k, w = state.shape
    return jnp.swapaxes(state, 0, 1).reshape(k * nb, w)


def _layer_prompt(x, p, rows, tt, keep, layer, depth, tails):
    B, T, D = x.shape
    lru_w, ssm_w, attn_w = p['widths'][0], p['widths'][1], p['widths'][2]
    xa, u, q, k, v, k_tail, v_tail = _in_proj(x, p['norm_pre_mix'], p['w_front'], p['widths'], rows,
                                              keep, layer, depth, tails)
    n_state = p['s5']['a_re'].shape[1]
    halo = (p['lru']['conv_w'].shape[0] - 1) * B
    ya, nconv, hlast = _rglru(xa, jnp.zeros((halo, lru_w), F32), jnp.zeros((B, lru_w), F32), p['lru'], B, tt)
    yb, s_re, s_im = _s5(u, jnp.zeros((B, n_state), F32), jnp.zeros((B, n_state), F32), p['s5'], B, tt)
    yc = _attn_prompt(q, k, v)
    x = _mix_mlp(x, ya, yb, yc, p, rows)
    return x, nconv, hlast, s_re, s_im, k_tail, v_tail


def _layer_sample(x, conv_state, h0, s0r, s0i, k_cache, v_cache, layer, p, nb, n_new):
    rows = x.shape[1]
    lru_w, ssm_w, attn_w = p['widths'][0], p['widths'][1], p['widths'][2]
    xa, u, q, k, v = _in_proj(x, p['norm_pre_mix'], p['w_front'], p['widths'], rows)
    ya, nconv, hlast = _rglru(xa[0], conv_state, h0, p['lru'], nb, n_new)
    yb, s_re, s_im = _s5(u[0], s0r, s0i, p['s5'], nb, n_new)
    pad = 8

    def bm(arr):
        arr = jnp.swapaxes(arr.reshape(n_new, nb, attn_w), 0, 1)
        return jnp.pad(arr, ((0, 0), (0, pad - n_new), (0, 0)))

    yc = _attn_sample(bm(q), bm(k), bm(v), k_cache, v_cache, layer, n_new)[:, :n_new]
    yc = jnp.swapaxes(yc, 0, 1).reshape(1, rows, attn_w)
    x = _mix_mlp(x, ya[None], yb[None], yc, p, rows)
    return x, nconv, hlast, s_re, s_im, k, v


def kernel(x_prompt, x_sample, state_conv, state_lru, state_ssm_re, state_ssm_im, cache_k, cache_v, norm_pre_mix, norm_post_mix, norm_pre_mlp, norm_post_mlp, w_in, conv_w, conv_b, lru_wr, lru_br, lru_wi, lru_bi, lru_lambda, ssm_lambda_re, ssm_lambda_im, ssm_log_dt, ssm_b_re, ssm_b_im, ssm_c_re, ssm_c_im, ssm_d, glu_w, glu_b, w_proj_a, w_proj_b, w_proj_c, w_out, mlp_w1, mlp_w2):
    a = dict(norm_pre_mix=norm_pre_mix, norm_post_mix=norm_post_mix, norm_pre_mlp=norm_pre_mlp,
             norm_post_mlp=norm_post_mlp, w_in=w_in, conv_w=conv_w, conv_b=conv_b, lru_wr=lru_wr, lru_br=lru_br,
             lru_wi=lru_wi, lru_bi=lru_bi, lru_lambda=lru_lambda, ssm_lambda_re=ssm_lambda_re,
             ssm_lambda_im=ssm_lambda_im, ssm_log_dt=ssm_log_dt, ssm_b_re=ssm_b_re, ssm_b_im=ssm_b_im,
             ssm_c_re=ssm_c_re, ssm_c_im=ssm_c_im, ssm_d=ssm_d, glu_w=glu_w, glu_b=glu_b, w_proj_a=w_proj_a,
             w_proj_b=w_proj_b, w_proj_c=w_proj_c, w_out=w_out, mlp_w1=mlp_w1, mlp_w2=mlp_w2,
             w_in_bf16=w_in.astype(BF16))
    depth = w_in.shape[0]
    Bp, Tp, D = x_prompt.shape
    Bs, Ts, _ = x_sample.shape
    G, N = state_ssm_re.shape[2:]
    attn_w = ATTN_HEADS * HEAD_DIM
    P = cache_k.shape[2]
    assert P == W_MAX, "the window buffer must hold exactly the widest window"
    keep = min(W_MAX, Tp)
    conv_k = state_conv.shape[2]

    xp = x_prompt
    xs = jnp.swapaxes(x_sample, 0, 1).reshape(1, Ts * Bs, D)
    k_rows = cache_k.reshape(depth, Bs, P, attn_w)
    v_rows = cache_v.reshape(depth, Bs, P, attn_w)
    outs = [[] for _ in range(12)]
    tails = ()
    for l in range(depth):
        p = _layer_params(l, a)
        xp, c_, h_, r_, i_, pk, pv = _layer_prompt(xp, p, rows=512, tt=64, keep=keep, layer=l, depth=depth,
                                                   tails=tails)
        tails = (pk, pv)
        for lst, val in zip(outs[:4], (jnp.swapaxes(c_.reshape(conv_k, Bp, -1), 0, 1), h_,
                                       r_.reshape(Bp, G, N), i_.reshape(Bp, G, N))):
            lst.append(val)
        xs, c_, h_, r_, i_, k_, v_ = _layer_sample(
            xs, _tm(state_conv[l]), state_lru[l], state_ssm_re[l].reshape(Bs, G * N),
            state_ssm_im[l].reshape(Bs, G * N), k_rows, v_rows, l, p, Bs, Ts)
        sk = jnp.swapaxes(k_.reshape(Ts, Bs, ATTN_HEADS, HEAD_DIM), 0, 1)
        sv = jnp.swapaxes(v_.reshape(Ts, Bs, ATTN_HEADS, HEAD_DIM), 0, 1)
        for lst, val in zip(outs[6:], (jnp.swapaxes(c_.reshape(conv_k, Bs, -1), 0, 1), h_,
                                       r_.reshape(Bs, G, N), i_.reshape(Bs, G, N), sk, sv)):
            lst.append(val)
    ys = jnp.swapaxes(xs.reshape(Ts, Bs, D), 0, 1)
    stacked = [jnp.stack(o) if o else None for o in outs]
    stacked[4], stacked[5] = tails
    return (xp, ys) + tuple(stacked)
```

```python
import functools
import math

import jax
import jax.numpy as jnp
from jax import lax
from jax.experimental import pallas as pl
from jax.experimental.pallas import tpu as pltpu

F32 = jnp.float32
BF16 = jnp.bfloat16

RMS_EPS = 1e-6
NEG_INF = -1e30
LOG2E = math.log2(math.e)
RG_C = 8.0
ATTN_HEADS = 12
HEAD_DIM = 64
DILATED_GROUPS = ((128, 1), (512, 4), (2048, 16))
W_MAX = max(w for w, _ in DILATED_GROUPS)
Q_BLOCK = 128

LANES = 128
MXU_TILE = 256
VMEM_LIMIT = 56 * 1024 * 1024


def _cparams(sem):
    return pltpu.CompilerParams(dimension_semantics=sem, vmem_limit_bytes=VMEM_LIMIT)


def _rms(x, g):
    return x * lax.rsqrt(jnp.mean(x * x, axis=-1, keepdims=True) + RMS_EPS) * g


def _dot(a, b):
    return jnp.dot(a, b, preferred_element_type=F32)


def _neg_expm1_2x(x, ex):
    e2 = ex * ex
    return jnp.where(jnp.abs(x) > 0.25, 1.0 - e2, -jnp.tanh(x) * (e2 + 1.0))


def _sigmoid(x):
    return 0.5 * jnp.tanh(0.5 * x) + 0.5


def _const_spec(shape):
    nd = len(shape)
    return pl.BlockSpec(shape, lambda *_: (0,) * nd)


def _resident_spec(shape):
    nd = len(shape)
    return pl.BlockSpec(shape, lambda *_: (0,) * nd, pipeline_mode=pl.Buffered(1))


def _in_proj_kernel(splits, tail_from, n_alias, x_ref, g_ref, w_ref, *refs):
    n = len(splits)
    out_refs = refs[n_alias:]
    h = _rms(x_ref[...], g_ref[...]).astype(BF16)
    for (lo, hi), o_ref in zip(splits, out_refs[:n]):
        o_ref[...] = _dot(h, w_ref[:, lo:hi])
    if len(out_refs) > n:
        @pl.when(pl.program_id(1) >= tail_from)
        def _():
            for src, dst in zip(out_refs[n - 2:n], out_refs[n:]):
                dst[...] = src[...].reshape(dst.shape)


def _in_proj(x, g, w, widths, rows, keep=0, layer=0, depth=1, tails=()):
    B, T, D = x.shape
    splits, lo = [], 0
    for wd in widths:
        splits.append((lo, lo + wd))
        lo += wd
    in_specs = [pl.BlockSpec((None, rows, D), lambda b, i: (b, i, 0)), _const_spec((1, D)), _const_spec(w.shape)]
    out_specs = [pl.BlockSpec((None, rows, wd), lambda b, i: (b, i, 0)) for wd in widths]
    out_shape = [jax.ShapeDtypeStruct((B, T, wd), F32) for wd in widths]
    tail_from = (T - keep) // rows
    aliases = {}
    if keep:
        assert keep % rows == 0 and widths[-1] == widths[-2] == ATTN_HEADS * HEAD_DIM
        tail = pl.BlockSpec((None, None, rows, ATTN_HEADS, HEAD_DIM),
                            lambda b, i: (layer, b, jnp.maximum(i - tail_from, 0), 0, 0))
        out_specs += [tail, tail]
        out_shape += [jax.ShapeDtypeStruct((depth, B, keep, ATTN_HEADS, HEAD_DIM), F32)] * 2
        in_specs += [pl.BlockSpec(memory_space=pl.ANY)] * len(tails)
        aliases = {len(in_specs) - len(tails) + j: len(widths) + j for j in range(len(tails))}
    return pl.pallas_call(
        functools.partial(_in_proj_kernel, tuple(splits), tail_from, len(tails)),
        grid=(B, T // rows),
        in_specs=in_specs,
        out_specs=out_specs,
        out_shape=out_shape,
        input_output_aliases=aliases,
        compiler_params=_cparams(("parallel", "arbitrary")),
        name="in_proj",
    )(x, g.reshape(1, D), w, *tails)


def _load_time_major(ref):
    if len(ref.shape) == 2:
        return ref[...]
    nb, tt, w = ref.shape
    return jnp.swapaxes(ref[...], 0, 1).reshape(tt * nb, w)


def _store_time_major(ref, val):
    if len(ref.shape) == 2:
        ref[...] = val.astype(ref.dtype)
    else:
        nb, tt, w = ref.shape
        ref[...] = jnp.swapaxes(val.reshape(tt, nb, w), 0, 1).astype(ref.dtype)


def _seq_specs(x, nb, tt):
    w = x.shape[-1]
    if x.ndim == 3:
        return pl.BlockSpec((nb, tt, w), lambda i: (0, i, 0)), x.shape[1]
    return pl.BlockSpec((tt * nb, w), lambda i: (i, 0)), x.shape[0] // nb


def _rglru_kernel(nb, tt, xa_ref, cs_ref, h0_ref, cw_ref, cb_ref, wr_ref, br_ref, wi_ref, bi_ref, lam_ref,
                  ya_ref, nconv_ref, hlast_ref, xe_buf, a_buf, b_buf, h_carry):
    rows = tt * nb
    halo = cs_ref.shape[0]
    n_tap = cw_ref.shape[0]

    @pl.when(pl.program_id(0) == 0)
    def _():
        xe_buf[0:halo] = cs_ref[...]
        h_carry[...] = h0_ref[...]

    xe_buf[halo:halo + rows] = _load_time_major(xa_ref)
    xc = cb_ref[...] + sum(cw_ref[i:i + 1, :] * xe_buf[i * nb:i * nb + rows] for i in range(n_tap))
    tail = xe_buf[rows:rows + halo]
    nconv_ref[...] = tail
    xe_buf[0:halo] = tail

    xcb = xc.astype(BF16)
    c = -RG_C * jax.nn.softplus(-lam_ref[...])
    for j in range(wr_ref.shape[0]):
        sl = slice(j * MXU_TILE, (j + 1) * MXU_TILE)
        r = _sigmoid(_dot(xcb[:, sl], wr_ref[j]) + br_ref[:, sl])
        ig = _sigmoid(_dot(xcb[:, sl], wi_ref[j]) + bi_ref[:, sl])
        log_a = c[:, sl] * r
        a = jnp.exp(log_a)
        a_buf[:, sl] = a
        mult = jnp.sqrt(jnp.maximum(_neg_expm1_2x(log_a, a), 0.0))
        b_buf[:, sl] = mult * (ig * xc[:, sl])

    def step(t, h):
        rs = pl.ds(pl.multiple_of(t * nb, nb), nb)
        h = a_buf[rs, :] * h + b_buf[rs, :]
        a_buf[rs, :] = h
        return h

    h = lax.fori_loop(0, tt, step, h_carry[...], unroll=True)
    h_carry[...] = h
    hlast_ref[...] = h
    _store_time_major(ya_ref, a_buf[...])


def _rglru(xa, conv_state, h0, p, nb, tt):
    W = xa.shape[-1]
    seq_spec, T = _seq_specs(xa, nb, tt)
    rows = tt * nb
    halo = conv_state.shape[0]
    assert rows >= halo and T % tt == 0
    consts = [p['conv_w'], p['conv_b'], p['wr'], p['br'], p['wi'], p['bi'], p['lam']]
    return pl.pallas_call(
        functools.partial(_rglru_kernel, nb, tt),
        grid=(T // tt,),
        in_specs=[seq_spec, _const_spec(conv_state.shape), _const_spec(h0.shape)]
                 + [_const_spec(a.shape) for a in consts],
        out_specs=[seq_spec, _const_spec(conv_state.shape), _const_spec(h0.shape)],
        out_shape=[jax.ShapeDtypeStruct(xa.shape, BF16), jax.ShapeDtypeStruct(conv_state.shape, F32),
                   jax.ShapeDtypeStruct(h0.shape, F32)],
        scratch_shapes=[pltpu.VMEM((rows + halo, W), F32), pltpu.VMEM((rows, W), F32),
                        pltpu.VMEM((rows, W), F32), pltpu.VMEM((nb, W), F32)],
        compiler_params=_cparams(("arbitrary",)),
        name="rglru",
    )(xa, conv_state, h0, *consts)


def _gelu_tanh(x):
    return 0.5 * x * (1.0 + jnp.tanh(math.sqrt(2.0 / math.pi) * (x + 0.044715 * (x * x * x))))


def _s5_kernel(nb, tt, chunk, u_ref, s0r_ref, s0i_ref, ar_ref, ai_ref, bre_ref, bim_ref, cre_ref, cim_ref,
               d_ref, gw_ref, gb_ref, yb_ref, lastr_ref, lasti_ref, sre, sim, car_r, car_i):
    n_state = sre.shape[1]
    n_half = bre_ref.shape[0]
    kw = bre_ref.shape[1]
    sw = bre_ref.shape[2]
    yw = cre_ref.shape[2]

    @pl.when(pl.program_id(0) == 0)
    def _():
        car_r[...] = s0r_ref[...]
        car_i[...] = s0i_ref[...]

    u = _load_time_major(u_ref)
    ub = u.astype(BF16)
    for hf in range(n_half):
        sre[:, hf * sw:(hf + 1) * sw] = _dot(ub[:, hf * kw:(hf + 1) * kw], bre_ref[hf])
        sim[:, hf * sw:(hf + 1) * sw] = _dot(ub[:, hf * kw:(hf + 1) * kw], bim_ref[hf])

    for c0 in range(0, n_state, chunk):
        cs = slice(c0, c0 + chunk)
        ar = jnp.broadcast_to(ar_ref[:, cs], (nb, chunk))
        ai = jnp.broadcast_to(ai_ref[:, cs], (nb, chunk))

        def step(t, carry):
            s_r, s_i = carry
            rs = pl.ds(pl.multiple_of(t * nb, nb), nb)
            n_r = ar * s_r - ai * s_i + sre[rs, cs]
            n_i = ar * s_i + ai * s_r + sim[rs, cs]
            sre[rs, cs] = n_r
            sim[rs, cs] = n_i
            return n_r, n_i

        s_r, s_i = lax.fori_loop(0, tt, step, (car_r[:, cs], car_i[:, cs]), unroll=True)
        car_r[:, cs] = s_r
        car_i[:, cs] = s_i

    lastr_ref[...] = car_r[...]
    lasti_ref[...] = car_i[...]

    ys = []
    for hf in range(n_half):
        ss = slice(hf * sw, (hf + 1) * sw)
        ys.append(_dot(sre[:, ss].astype(BF16), cre_ref[hf]) - _dot(sim[:, ss].astype(BF16), cim_ref[hf]))
    y = jnp.concatenate(ys, axis=1) + d_ref[...] * u
    z = _gelu_tanh(y).astype(BF16)
    w_half = gw_ref.shape[1] // 2
    val = _dot(z, gw_ref[:, :w_half]) + gb_ref[:, :w_half]
    gate = _dot(z, gw_ref[:, w_half:]) + gb_ref[:, w_half:]
    _store_time_major(yb_ref, val * jax.nn.sigmoid(gate))


def _s5(u, s0r, s0i, p, nb, tt):
    seq_spec, T = _seq_specs(u, nb, tt)
    rows = tt * nb
    n_state = s0r.shape[1]
    chunk = max(LANES, min(n_state, 4 * LANES * 8 // nb))
    consts = [p['a_re'], p['a_im'], p['b_re'], p['b_im'], p['c_re'], p['c_im'], p['d'], p['glu_w'], p['glu_b']]
    return pl.pallas_call(
        functools.partial(_s5_kernel, nb, tt, chunk),
        grid=(T // tt,),
        in_specs=[seq_spec, _const_spec(s0r.shape), _const_spec(s0i.shape)]
                 + [_const_spec(a.shape) for a in consts],
        out_specs=[seq_spec, _const_spec(s0r.shape), _const_spec(s0i.shape)],
        out_shape=[jax.ShapeDtypeStruct(u.shape, BF16), jax.ShapeDtypeStruct(s0r.shape, F32),
                   jax.ShapeDtypeStruct(s0i.shape, F32)],
        scratch_shapes=[pltpu.VMEM((rows, n_state), F32), pltpu.VMEM((rows, n_state), F32),
                        pltpu.VMEM((nb, n_state), F32), pltpu.VMEM((nb, n_state), F32)],
        compiler_params=_cparams(("arbitrary",)),
        name="s5",
    )(u, s0r, s0i, *consts)


def _attn_prompt_kernel(tq, q_ref, kp_ref, kc_ref, vp_ref, vc_ref, o_ref,
                        acc, mst, lst, mask_t, one_hot, sbuf, pbuf):
    it = pl.program_id(2)
    qb = Q_BLOCK
    scale = HEAD_DIM ** -0.5
    lane = lax.broadcasted_iota(jnp.int32, (1, LANES), 1)
    head0 = lane < HEAD_DIM
    key = lax.broadcasted_iota(jnp.int32, (2 * qb, qb), 0)
    row = lax.broadcasted_iota(jnp.int32, (2 * qb, qb), 1)
    band_t = (key >= row) & (key <= row + qb)
    mask_t[0] = jnp.where(band_t, 0.0, NEG_INF).astype(BF16)
    mask_t[1] = jnp.where(band_t & (key >= qb), 0.0, NEG_INF).astype(BF16)
    one_hot[...] = (key % qb == row).astype(BF16)

    blocks = []
    for g, (w, d) in enumerate(DILATED_GROUPS):
        assert w // d == qb
        nblk = tq // (qb * d)
        for idx in range(d * nblk):
            qstart = idx // nblk + d * qb * (idx % nblk)
            kstart = qstart - d * qb
            stride = {} if d == 1 else {"stride": d}
            if kstart < 0:
                ks = (pl.ds(tq + kstart, qb, **stride), pl.ds(qstart, qb, **stride))
            else:
                ks = (pl.ds(kstart, 2 * qb, **stride),)
            blocks.append((g, kstart < 0, pl.ds(qstart, qb, **stride), ks))

    def key_rows(prev_ref, cur_ref, ks):
        if len(ks) == 1:
            return cur_ref[ks[0], :].astype(BF16)
        return jnp.concatenate([prev_ref[ks[0], :], cur_ref[ks[1], :]], axis=0).astype(BF16)

    n_slot = sbuf.shape[0]
    first_tile = (it == 0).astype(jnp.int32)

    def scores(j):
        _, before_tile, qs, ks = blocks[j]
        qt = q_ref[qs, :] * (scale * LOG2E)
        q2 = jnp.concatenate([jnp.where(head0, qt, 0.0), jnp.where(head0, 0.0, qt)], axis=0).astype(BF16)
        q_aug = jnp.concatenate([q2, one_hot[...]], axis=1)
        k_aug = jnp.concatenate([key_rows(kp_ref, kc_ref, ks), mask_t[first_tile] if before_tile else mask_t[0]],
                                axis=1)
        sbuf[j % n_slot] = lax.dot_general(q_aug, k_aug, (((1,), (1,)), ((), ())), preferred_element_type=F32)

    def probs(j):
        g, _, qs, _ = blocks[j]
        sl = j % n_slot
        m = jnp.max(jnp.maximum(sbuf[sl, :, 0:LANES], sbuf[sl, :, LANES:2 * LANES]), axis=-1, keepdims=True)
        pbuf[sl, :, 0:LANES] = jnp.exp2(sbuf[sl, :, 0:LANES] - m).astype(BF16)
        pbuf[sl, :, LANES:2 * LANES] = jnp.exp2(sbuf[sl, :, LANES:2 * LANES] - m).astype(BF16)
        mst[g, qs, :] = jnp.where(head0, m[0:qb], m[qb:2 * qb])

    def values(j):
        g, _, qs, ks = blocks[j]
        v_aug = jnp.concatenate([key_rows(vp_ref, vc_ref, ks), jnp.ones((2 * qb, LANES), BF16)], axis=1)
        pv = _dot(pbuf[j % n_slot], v_aug)
        acc[g, qs, :] = jnp.where(head0, pv[0:qb, 0:LANES], pv[qb:2 * qb, 0:LANES])
        lst[g, qs, :] = jnp.where(head0, pv[0:qb, LANES:2 * LANES], pv[qb:2 * qb, LANES:2 * LANES])

    group = 2
    n_chunk = len(blocks) // group
    for step in range(n_chunk + 2):
        @pl.when(it >= 0)
        def _(step=step):
            for c, stage in ((step - 2, values), (step - 1, probs), (step, scores)):
                if 0 <= c < n_chunk:
                    for j in range(c * group, (c + 1) * group):
                        stage(j)

    n_g = len(DILATED_GROUPS)
    m_top = functools.reduce(jnp.maximum, [mst[g] for g in range(n_g)])
    num = 0.0
    den = 0.0
    for g in range(n_g):
        wgt = jnp.exp2(mst[g] - m_top)
        num = num + wgt * acc[g]
        den = den + wgt * lst[g]
    o_ref[...] = (num / den).astype(o_ref.dtype)


def _attn_prompt(q, k, v):
    B, T, W = q.shape
    tq = W_MAX
    assert T % tq == 0 and W % LANES == 0
    n_hp = W // LANES
    cur = pl.BlockSpec((None, tq, LANES), lambda b, h, i: (b, i, h))
    prev = pl.BlockSpec((None, tq, LANES), lambda b, h, i: (b, jnp.maximum(i - 1, 0), h))
    n_g = len(DILATED_GROUPS)
    n_blk, qb2 = 12, 2 * Q_BLOCK
    return pl.pallas_call(
        functools.partial(_attn_prompt_kernel, tq),
        grid=(B, n_hp, T // tq),
        in_specs=[cur, prev, cur, prev, cur],
        out_specs=cur,
        out_shape=jax.ShapeDtypeStruct((B, T, W), BF16),
        scratch_shapes=[pltpu.VMEM((n_g, tq, LANES), F32), pltpu.VMEM((n_g, tq, LANES), F32),
                        pltpu.VMEM((n_g, tq, LANES), F32),
                        pltpu.VMEM((2, qb2, Q_BLOCK), BF16), pltpu.VMEM((qb2, Q_BLOCK), BF16),
                        pltpu.VMEM((n_blk, qb2, qb2), F32), pltpu.VMEM((n_blk, qb2, qb2), BF16)],
        compiler_params=_cparams(("parallel", "parallel", "arbitrary")),
        name="attn_prompt",
    )(q, k, k, v, v)


def _attn_sample_kernel(n_new, q_ref, kc_ref, vc_ref, kn_ref, vn_ref, o_ref):
    P = kc_ref.shape[0]
    W = kc_ref.shape[1]
    pad = q_ref.shape[0]
    n_rows = ATTN_HEADS * pad
    scale = HEAD_DIM ** -0.5
    q = q_ref[...]
    col_head = lax.broadcasted_iota(jnp.int32, (pad, W), 1) // HEAD_DIM
    qbd = jnp.concatenate([jnp.where(col_head == h, q, 0.0) for h in range(ATTN_HEADS)], axis=0).astype(BF16)

    def multiplicity(dist):
        cnt = jnp.zeros(dist.shape, F32)
        for w, d in DILATED_GROUPS:
            cnt = cnt + ((dist >= 0) & (dist <= w) & (dist % d == 0)).astype(F32)
        return cnt

    nt = (((1,), (1,)), ((), ()))
    s_c = lax.dot_general(qbd, kc_ref[...].astype(BF16), nt, preferred_element_type=F32) * scale
    s_n = lax.dot_general(qbd, kn_ref[...].astype(BF16), nt, preferred_element_type=F32) * scale
    j_c = lax.broadcasted_iota(jnp.int32, (n_rows, P), 0) % pad
    i_c = lax.broadcasted_iota(jnp.int32, (n_rows, P), 1)
    mult_c = multiplicity(j_c + P - i_c)
    j_n = lax.broadcasted_iota(jnp.int32, (n_rows, pad), 0) % pad
    i_n = lax.broadcasted_iota(jnp.int32, (n_rows, pad), 1)
    mult_n = jnp.where(i_n < n_new, multiplicity(j_n - i_n), 0.0)
    s_c = jnp.where(mult_c > 0, s_c, NEG_INF)
    s_n = jnp.where(mult_n > 0, s_n, NEG_INF)
    m = jnp.maximum(jnp.max(s_c, axis=-1, keepdims=True), jnp.max(s_n, axis=-1, keepdims=True))
    p_c = mult_c * jnp.exp(s_c - m)
    p_n = mult_n * jnp.exp(s_n - m)
    l = jnp.sum(p_c, axis=-1, keepdims=True) + jnp.sum(p_n, axis=-1, keepdims=True)
    pv = (_dot(p_c.astype(BF16), vc_ref[...].astype(BF16)) + _dot(p_n.astype(BF16), vn_ref[...].astype(BF16))) / l
    out = jnp.zeros((pad, W), F32)
    for h in range(ATTN_HEADS):
        out = out + jnp.where(col_head == h, pv[h * pad:(h + 1) * pad], 0.0)
    o_ref[...] = out


def _attn_sample(q, k_new, v_new, k_cache, v_cache, layer, n_new):
    B, pad, W = q.shape
    P = k_cache.shape[2]
    small = pl.BlockSpec((None, pad, W), lambda b: (b, 0, 0))
    big = pl.BlockSpec((None, None, P, W), lambda b: (layer, b, 0, 0))
    return pl.pallas_call(
        functools.partial(_attn_sample_kernel, n_new),
        grid=(B,),
        in_specs=[small, big, big, small, small],
        out_specs=small,
        out_shape=jax.ShapeDtypeStruct((B, pad, W), F32),
        compiler_params=_cparams(("parallel",)),
        name="attn_sample",
    )(q, k_cache, v_cache, k_new, v_new)


def _mix_mlp_kernel(ff_chunk, x_ref, gpre_ref, wg_ref, ya_ref, yb_ref, yc_ref, wa_ref, wb_ref, wc_ref, wo_ref,
                    gpost_ref, g2pre_ref, w1_ref, w2_ref, g2post_ref, o_ref):
    x = x_ref[...]
    D = x.shape[1]
    h = _rms(x, gpre_ref[...]).astype(BF16)
    merged = None
    for i, (y_ref, w_ref) in enumerate(((ya_ref, wa_ref), (yb_ref, wb_ref), (yc_ref, wc_ref))):
        gate = jax.nn.sigmoid(_dot(h, wg_ref[:, i * D:(i + 1) * D]))
        term = gate * _dot(y_ref[...].astype(BF16), w_ref[...])
        merged = term if merged is None else merged + term
    x = x + _rms(_dot(merged.astype(BF16), wo_ref[...]), gpost_ref[...])

    h = _rms(x, g2pre_ref[...]).astype(BF16)
    acc = None
    for c0 in range(0, w1_ref.shape[1], ff_chunk):
        hid = jnp.square(jnp.maximum(_dot(h, w1_ref[:, c0:c0 + ff_chunk]), 0.0)).astype(BF16)
        part = _dot(hid, w2_ref[c0:c0 + ff_chunk, :])
        acc = part if acc is None else acc + part
    o_ref[...] = x + _rms(acc, g2post_ref[...])


def _mix_mlp(x, ya, yb, yc, p, rows):
    B, T, D = x.shape
    spec = lambda a: pl.BlockSpec((None, rows, a.shape[2]), lambda b, i: (b, i, 0))
    consts = [p['norm_pre_mix'], p['w_gates']]
    tail = [p['w_proj_a'], p['w_proj_b'], p['w_proj_c'], p['w_out'], p['norm_post_mix'],
            p['norm_pre_mlp'], p['mlp_w1'], p['mlp_w2'], p['norm_post_mlp']]
    return pl.pallas_call(
        functools.partial(_mix_mlp_kernel, 1024),
        grid=(B, T // rows),
        in_specs=[spec(x)] + [_resident_spec(a.shape) for a in consts] + [spec(ya), spec(yb), spec(yc)]
                 + [_resident_spec(a.shape) for a in tail],
        out_specs=spec(x),
        out_shape=jax.ShapeDtypeStruct((B, T, D), F32),
        compiler_params=_cparams(("parallel", "parallel")),
        name="mix_mlp",
    )(x, *consts, ya, yb, yc, *tail)


def _block_diag_tiles(w, per_tile):
    n, c, d = w.shape
    w = w.reshape(n // per_tile, per_tile, c, d)
    eye = jnp.eye(per_tile, dtype=w.dtype)
    return jnp.einsum('jpcd,pq->jpcqd', w, eye).reshape(n // per_tile, per_tile * c, per_tile * d)


def _layer_params(l, a):
    D = a['w_in'].shape[1]
    lru_w = a['conv_w'].shape[2]
    n_blk, blk = a['lru_wr'].shape[1], a['lru_wr'].shape[2]
    G, N, GC = a['ssm_b_re'].shape[1:]
    ssm_w = G * GC
    attn_w = ATTN_HEADS * HEAD_DIM
    n_front = lru_w + ssm_w + 3 * attn_w
    row = lambda v: v.reshape(1, -1).astype(F32)
    p = {
        'norm_pre_mix': row(a['norm_pre_mix'][l]), 'norm_post_mix': row(a['norm_post_mix'][l]),
        'norm_pre_mlp': row(a['norm_pre_mlp'][l]), 'norm_post_mlp': row(a['norm_post_mlp'][l]),
        'w_front': a['w_in_bf16'][l][:, :n_front], 'w_gates': a['w_in_bf16'][l][:, n_front:],
        'w_proj_a': a['w_proj_a'][l].astype(BF16), 'w_proj_b': a['w_proj_b'][l].astype(BF16),
        'w_proj_c': a['w_proj_c'][l].astype(BF16), 'w_out': a['w_out'][l].astype(BF16),
        'mlp_w1': a['mlp_w1'][l].astype(BF16), 'mlp_w2': a['mlp_w2'][l].astype(BF16),
        'widths': (lru_w, ssm_w, attn_w, attn_w, attn_w),
    }
    per_tile = MXU_TILE // blk
    p['lru'] = {
        'conv_w': a['conv_w'][l].astype(F32), 'conv_b': row(a['conv_b'][l]),
        'wr': _block_diag_tiles(a['lru_wr'][l], per_tile).astype(BF16), 'br': row(a['lru_br'][l]),
        'wi': _block_diag_tiles(a['lru_wi'][l], per_tile).astype(BF16), 'bi': row(a['lru_bi'][l]),
        'lam': row(a['lru_lambda'][l]),
    }
    lr, li = a['ssm_lambda_re'][l].astype(F32), a['ssm_lambda_im'][l].astype(F32)
    dt = jnp.exp(a['ssm_log_dt'][l].astype(F32))[:, None]
    mag = jnp.exp(lr * dt)
    ang = li * dt
    ab_re, ab_im = mag * jnp.cos(ang), mag * jnp.sin(ang)
    den = lr * lr + li * li
    f_re = ((ab_re - 1.0) * lr + ab_im * li) / den
    f_im = (ab_im * lr - (ab_re - 1.0) * li) / den
    b_re, b_im = a['ssm_b_re'][l].astype(F32), a['ssm_b_im'][l].astype(F32)
    bb_re = f_re[..., None] * b_re - f_im[..., None] * b_im
    bb_im = f_re[..., None] * b_im + f_im[..., None] * b_re
    g_half = G // 2
    to_b = lambda m: _block_diag_tiles(jnp.swapaxes(m, 1, 2), g_half).astype(BF16)
    to_c = lambda m: _block_diag_tiles(jnp.swapaxes(m, 1, 2), g_half).astype(BF16)
    p['s5'] = {
        'a_re': ab_re.reshape(1, G * N), 'a_im': ab_im.reshape(1, G * N),
        'b_re': to_b(bb_re), 'b_im': to_b(bb_im),
        'c_re': to_c(a['ssm_c_re'][l].astype(F32)), 'c_im': to_c(a['ssm_c_im'][l].astype(F32)),
        'd': row(a['ssm_d'][l]), 'glu_w': a['glu_w'][l].astype(BF16), 'glu_b': row(a['glu_b'][l]),
    }
    return p


def _tm(state):
    nb, k, w = state.shape
    return jnp.swapaxes(state, 0, 1).reshape(k * nb, w)


def _layer_prompt(x, p, rows, tt, keep, layer, depth, tails):
    B, T, D = x.shape
    lru_w, ssm_w, attn_w = p['widths'][0], p['widths'][1], p['widths'][2]
    xa, u, q, k, v, k_tail, v_tail = _in_proj(x, p['norm_pre_mix'], p['w_front'], p['widths'], rows,
                                              keep, layer, depth, tails)
    n_state = p['s5']['a_re'].shape[1]
    halo = (p['lru']['conv_w'].shape[0] - 1) * B
    ya, nconv, hlast = _rglru(xa, jnp.zeros((halo, lru_w), F32), jnp.zeros((B, lru_w), F32), p['lru'], B, tt)
    yb, s_re, s_im = _s5(u, jnp.zeros((B, n_state), F32), jnp.zeros((B, n_state), F32), p['s5'], B, tt)
    yc = _attn_prompt(q, k, v)
    x = _mix_mlp(x, ya, yb, yc, p, rows)
    return x, nconv, hlast, s_re, s_im, k_tail, v_tail


def _layer_sample(x, conv_state, h0, s0r, s0i, k_cache, v_cache, layer, p, nb, n_new):
    rows = x.shape[1]
    lru_w, ssm_w, attn_w = p['widths'][0], p['widths'][1], p['widths'][2]
    xa, u, q, k, v = _in_proj(x, p['norm_pre_mix'], p['w_front'], p['widths'], rows)
    ya, nconv, hlast = _rglru(xa[0], conv_state, h0, p['lru'], nb, n_new)
    yb, s_re, s_im = _s5(u[0], s0r, s0i, p['s5'], nb, n_new)
    pad = 8

    def bm(arr):
        arr = jnp.swapaxes(arr.reshape(n_new, nb, attn_w), 0, 1)
        return jnp.pad(arr, ((0, 0), (0, pad - n_new), (0, 0)))

    yc = _attn_sample(bm(q), bm(k), bm(v), k_cache, v_cache, layer, n_new)[:, :n_new]
    yc = jnp.swapaxes(yc, 0, 1).reshape(1, rows, attn_w)
    x = _mix_mlp(x, ya[None], yb[None], yc, p, rows)
    return x, nconv, hlast, s_re, s_im, k, v


def kernel(x_prompt, x_sample, state_conv, state_lru, state_ssm_re, state_ssm_im, cache_k, cache_v, norm_pre_mix, norm_post_mix, norm_pre_mlp, norm_post_mlp, w_in, conv_w, conv_b, lru_wr, lru_br, lru_wi, lru_bi, lru_lambda, ssm_lambda_re, ssm_lambda_im, ssm_log_dt, ssm_b_re, ssm_b_im, ssm_c_re, ssm_c_im, ssm_d, glu_w, glu_b, w_proj_a, w_proj_b, w_proj_c, w_out, mlp_w1, mlp_w2):
    a = dict(norm_pre_mix=norm_pre_mix, norm_post_mix=norm_post_mix, norm_pre_mlp=norm_pre_mlp,
             norm_post_mlp=norm_post_mlp, w_in=w_in, conv_w=conv_w, conv_b=conv_b, lru_wr=lru_wr, lru_br=lru_br,
             lru_wi=lru_wi, lru_bi=lru_bi, lru_lambda=lru_lambda, ssm_lambda_re=ssm_lambda_re,
             ssm_lambda_im=ssm_lambda_im, ssm_log_dt=ssm_log_dt, ssm_b_re=ssm_b_re, ssm_b_im=ssm_b_im,
             ssm_c_re=ssm_c_re, ssm_c_im=ssm_c_im, ssm_d=ssm_d, glu_w=glu_w, glu_b=glu_b, w_proj_a=w_proj_a,
             w_proj_b=w_proj_b, w_proj_c=w_proj_c, w_out=w_out, mlp_w1=mlp_w1, mlp_w2=mlp_w2,
             w_in_bf16=w_in.astype(BF16))
    depth = w_in.shape[0]
    Bp, Tp, D = x_prompt.shape
    Bs, Ts, _ = x_sample.shape
    G, N = state_ssm_re.shape[2:]
    attn_w = ATTN_HEADS * HEAD_DIM
    P = cache_k.shape[2]
    assert P == W_MAX, "the window buffer must hold exactly the widest window"
    keep = min(W_MAX, Tp)
    conv_k = state_conv.shape[2]

    xp = x_prompt
    xs = jnp.swapaxes(x_sample, 0, 1).reshape(1, Ts * Bs, D)
    k_rows = cache_k.reshape(depth, Bs, P, attn_w)
    v_rows = cache_v.reshape(depth, Bs, P, attn_w)
    outs = [[] for _ in range(12)]
    tails = ()
    for l in range(depth):
        p = _layer_params(l, a)
        xp, c_, h_, r_, i_, pk, pv = _layer_prompt(xp, p, rows=512, tt=64, keep=keep, layer=l, depth=depth,
                                                   tails=tails)
        tails = (pk, pv)
        for lst, val in zip(outs[:4], (jnp.swapaxes(c_.reshape(conv_k, Bp, -1), 0, 1), h_,
                                       r_.reshape(Bp, G, N), i_.reshape(Bp, G, N))):
            lst.append(val)
        xs, c_, h_, r_, i_, k_, v_ = _layer_sample(
            xs, _tm(state_conv[l]), state_lru[l], state_ssm_re[l].reshape(Bs, G * N),
            state_ssm_im[l].reshape(Bs, G * N), k_rows, v_rows, l, p, Bs, Ts)
        sk = jnp.swapaxes(k_.reshape(Ts, Bs, ATTN_HEADS, HEAD_DIM), 0, 1)
        sv = jnp.swapaxes(v_.reshape(Ts, Bs, ATTN_HEADS, HEAD_DIM), 0, 1)
        for lst, val in zip(outs[6:], (jnp.swapaxes(c_.reshape(conv_k, Bs, -1), 0, 1), h_,
                                       r_.reshape(Bs, G, N), i_.reshape(Bs, G, N), sk, sv)):
            lst.append(val)
    ys = jnp.swapaxes(xs.reshape(Ts, Bs, D), 0, 1)
    stacked = [jnp.stack(o) if o else None for o in outs]
    stacked[4], stacked[5] = tails
    return (xp, ys) + tuple(stacked)
```

```python
import functools
import math

import jax
import jax.numpy as jnp
from jax import lax
from jax.experimental import pallas as pl
from jax.experimental.pallas import tpu as pltpu

F32 = jnp.float32
BF16 = jnp.bfloat16

RMS_EPS = 1e-6
NEG_INF = -1e30
LOG2E = math.log2(math.e)
RG_C = 8.0
ATTN_HEADS = 12
HEAD_DIM = 64
DILATED_GROUPS = ((128, 1), (512, 4), (2048, 16))
W_MAX = max(w for w, _ in DILATED_GROUPS)
Q_BLOCK = 128

LANES = 128
MXU_TILE = 256
VMEM_LIMIT = 56 * 1024 * 1024

ROW_BLOCK = 512
TIME_BLOCK = 64
FF_CHUNK = 1024
ATTN_GROUP = 2
ATTN_SLOTS = 6 * ATTN_GROUP


def _cparams(sem):
    return pltpu.CompilerParams(dimension_semantics=sem, vmem_limit_bytes=VMEM_LIMIT)


def _rms(x, g):
    return x * lax.rsqrt(jnp.mean(x * x, axis=-1, keepdims=True) + RMS_EPS) * g


def _dot(a, b):
    return jnp.dot(a, b, preferred_element_type=F32)


def _neg_expm1_2x(x, ex):
    e2 = ex * ex
    return jnp.where(jnp.abs(x) > 0.25, 1.0 - e2, -jnp.tanh(x) * (e2 + 1.0))


def _sigmoid(x):
    return 0.5 * jnp.tanh(0.5 * x) + 0.5


def _const_spec(shape):
    nd = len(shape)
    return pl.BlockSpec(shape, lambda *_: (0,) * nd)


def _resident_spec(shape):
    nd = len(shape)
    return pl.BlockSpec(shape, lambda *_: (0,) * nd, pipeline_mode=pl.Buffered(1))


def _in_proj_kernel(splits, n_alias, x_ref, g_ref, w_ref, *refs):
    n = len(splits)
    out_refs = refs[n_alias:]
    h = _rms(x_ref[...], g_ref[...]).astype(BF16)
    for (lo, hi), o_ref in zip(splits, out_refs[:n]):
        o_ref[...] = _dot(h, w_ref[:, lo:hi])
    if len(out_refs) > n:
        for src, dst in zip(out_refs[n - 2:n], out_refs[n:]):
            dst[...] = src[...].reshape(dst.shape)


def _in_proj(x, g, w, widths, rows, keep=0, layer=0, depth=1, tails=()):
    B, T, D = x.shape
    splits, lo = [], 0
    for wd in widths:
        splits.append((lo, lo + wd))
        lo += wd
    in_specs = [pl.BlockSpec((None, rows, D), lambda b, i: (b, i, 0)), _const_spec((1, D)), _const_spec(w.shape)]
    out_specs = [pl.BlockSpec((None, rows, wd), lambda b, i: (b, i, 0)) for wd in widths]
    out_shape = [jax.ShapeDtypeStruct((B, T, wd), F32) for wd in widths]
    tail_from = (T - keep) // rows
    aliases = {}
    if keep:
        assert keep % rows == 0 and widths[-1] == widths[-2] == ATTN_HEADS * HEAD_DIM
        tail = pl.BlockSpec((None, None, rows, ATTN_HEADS, HEAD_DIM),
                            lambda b, i: (layer, b, jnp.maximum(i - tail_from, 0), 0, 0))
        out_specs += [tail, tail]
        out_shape += [jax.ShapeDtypeStruct((depth, B, keep, ATTN_HEADS, HEAD_DIM), F32)] * 2
        in_specs += [pl.BlockSpec(memory_space=pl.ANY)] * len(tails)
        aliases = {len(in_specs) - len(tails) + j: len(widths) + j for j in range(len(tails))}
    return pl.pallas_call(
        functools.partial(_in_proj_kernel, tuple(splits), len(tails)),
        grid=(B, T // rows),
        in_specs=in_specs,
        out_specs=out_specs,
        out_shape=out_shape,
        input_output_aliases=aliases,
        compiler_params=_cparams(("parallel", "arbitrary")),
        name="in_proj",
    )(x, g.reshape(1, D), w, *tails)


def _load_time_major(ref):
    if len(ref.shape) == 2:
        return ref[...]
    nb, tt, w = ref.shape
    return jnp.swapaxes(ref[...], 0, 1).reshape(tt * nb, w)


def _store_time_major(ref, val):
    if len(ref.shape) == 2:
        ref[...] = val.astype(ref.dtype)
    else:
        nb, tt, w = ref.shape
        ref[...] = jnp.swapaxes(val.reshape(tt, nb, w), 0, 1).astype(ref.dtype)


def _seq_specs(x, nb, tt):
    w = x.shape[-1]
    if x.ndim == 3:
        return pl.BlockSpec((nb, tt, w), lambda i: (0, i, 0)), x.shape[1]
    return pl.BlockSpec((tt * nb, w), lambda i: (i, 0)), x.shape[0] // nb


def _rglru_kernel(nb, tt, xa_ref, cs_ref, h0_ref, cw_ref, cb_ref, wr_ref, br_ref, wi_ref, bi_ref, lam_ref,
                  ya_ref, nconv_ref, hlast_ref, xe_buf, a_buf, b_buf, h_carry):
    rows = tt * nb
    halo = cs_ref.shape[0]
    n_tap = cw_ref.shape[0]

    @pl.when(pl.program_id(0) == 0)
    def _():
        xe_buf[0:halo] = cs_ref[...]
        h_carry[...] = h0_ref[...]

    xe_buf[halo:halo + rows] = _load_time_major(xa_ref)
    xc = cb_ref[...] + sum(cw_ref[i:i + 1, :] * xe_buf[i * nb:i * nb + rows] for i in range(n_tap))
    tail = xe_buf[rows:rows + halo]
    nconv_ref[...] = tail
    xe_buf[0:halo] = tail

    xcb = xc.astype(BF16)
    c = -RG_C * jax.nn.softplus(-lam_ref[...])
    for j in range(wr_ref.shape[0]):
        sl = slice(j * MXU_TILE, (j + 1) * MXU_TILE)
        r = _sigmoid(_dot(xcb[:, sl], wr_ref[j]) + br_ref[:, sl])
        ig = _sigmoid(_dot(xcb[:, sl], wi_ref[j]) + bi_ref[:, sl])
        log_a = c[:, sl] * r
        a = jnp.exp(log_a)
        a_buf[:, sl] = a
        mult = jnp.sqrt(jnp.maximum(_neg_expm1_2x(log_a, a), 0.0))
        b_buf[:, sl] = mult * (ig * xc[:, sl])

    def step(t, h):
        rs = pl.ds(pl.multiple_of(t * nb, nb), nb)
        h = a_buf[rs, :] * h + b_buf[rs, :]
        a_buf[rs, :] = h
        return h

    h = lax.fori_loop(0, tt, step, h_carry[...], unroll=True)
    h_carry[...] = h
    hlast_ref[...] = h
    _store_time_major(ya_ref, a_buf[...])


def _rglru(xa, conv_state, h0, p, nb, tt):
    W = xa.shape[-1]
    seq_spec, T = _seq_specs(xa, nb, tt)
    rows = tt * nb
    halo = conv_state.shape[0]
    assert rows >= halo and T % tt == 0
    consts = [p['conv_w'], p['conv_b'], p['wr'], p['br'], p['wi'], p['bi'], p['lam']]
    return pl.pallas_call(
        functools.partial(_rglru_kernel, nb, tt),
        grid=(T // tt,),
        in_specs=[seq_spec, _const_spec(conv_state.shape), _const_spec(h0.shape)]
                 + [_const_spec(a.shape) for a in consts],
        out_specs=[seq_spec, _const_spec(conv_state.shape), _const_spec(h0.shape)],
        out_shape=[jax.ShapeDtypeStruct(xa.shape, BF16), jax.ShapeDtypeStruct(conv_state.shape, F32),
                   jax.ShapeDtypeStruct(h0.shape, F32)],
        scratch_shapes=[pltpu.VMEM((rows + halo, W), F32), pltpu.VMEM((rows, W), F32),
                        pltpu.VMEM((rows, W), F32), pltpu.VMEM((nb, W), F32)],
        compiler_params=_cparams(("arbitrary",)),
        name="rglru",
    )(xa, conv_state, h0, *consts)


def _gelu_tanh(x):
    return 0.5 * x * (1.0 + jnp.tanh(math.sqrt(2.0 / math.pi) * (x + 0.044715 * (x * x * x))))


def _s5_kernel(nb, tt, chunk, u_ref, s0r_ref, s0i_ref, ar_ref, ai_ref, bre_ref, bim_ref, cre_ref, cim_ref,
               d_ref, gw_ref, gb_ref, yb_ref, lastr_ref, lasti_ref, sre, sim, car_r, car_i):
    n_state = sre.shape[1]
    n_half = bre_ref.shape[0]
    kw = bre_ref.shape[1]
    sw = bre_ref.shape[2]
    yw = cre_ref.shape[2]

    @pl.when(pl.program_id(0) == 0)
    def _():
        car_r[...] = s0r_ref[...]
        car_i[...] = s0i_ref[...]

    u = _load_time_major(u_ref)
    ub = u.astype(BF16)
    for hf in range(n_half):
        sre[:, hf * sw:(hf + 1) * sw] = _dot(ub[:, hf * kw:(hf + 1) * kw], bre_ref[hf])
        sim[:, hf * sw:(hf + 1) * sw] = _dot(ub[:, hf * kw:(hf + 1) * kw], bim_ref[hf])

    for c0 in range(0, n_state, chunk):
        cs = slice(c0, c0 + chunk)
        ar = jnp.broadcast_to(ar_ref[:, cs], (nb, chunk))
        ai = jnp.broadcast_to(ai_ref[:, cs], (nb, chunk))

        def step(t, carry):
            s_r, s_i = carry
            rs = pl.ds(pl.multiple_of(t * nb, nb), nb)
            n_r = ar * s_r - ai * s_i + sre[rs, cs]
            n_i = ar * s_i + ai * s_r + sim[rs, cs]
            sre[rs, cs] = n_r
            sim[rs, cs] = n_i
            return n_r, n_i

        s_r, s_i = lax.fori_loop(0, tt, step, (car_r[:, cs], car_i[:, cs]), unroll=True)
        car_r[:, cs] = s_r
        car_i[:, cs] = s_i

    lastr_ref[...] = car_r[...]
    lasti_ref[...] = car_i[...]

    ys = []
    for hf in range(n_half):
        ss = slice(hf * sw, (hf + 1) * sw)
        ys.append(_dot(sre[:, ss].astype(BF16), cre_ref[hf]) - _dot(sim[:, ss].astype(BF16), cim_ref[hf]))
    y = jnp.concatenate(ys, axis=1) + d_ref[...] * u
    z = _gelu_tanh(y).astype(BF16)
    w_half = gw_ref.shape[1] // 2
    val = _dot(z, gw_ref[:, :w_half]) + gb_ref[:, :w_half]
    gate = _dot(z, gw_ref[:, w_half:]) + gb_ref[:, w_half:]
    _store_time_major(yb_ref, val * jax.nn.sigmoid(gate))


def _s5(u, s0r, s0i, p, nb, tt):
    seq_spec, T = _seq_specs(u, nb, tt)
    rows = tt * nb
    n_state = s0r.shape[1]
    chunk = max(LANES, min(n_state, 4 * LANES * 8 // nb))
    consts = [p['a_re'], p['a_im'], p['b_re'], p['b_im'], p['c_re'], p['c_im'], p['d'], p['glu_w'], p['glu_b']]
    return pl.pallas_call(
        functools.partial(_s5_kernel, nb, tt, chunk),
        grid=(T // tt,),
        in_specs=[seq_spec, _const_spec(s0r.shape), _const_spec(s0i.shape)]
                 + [_const_spec(a.shape) for a in consts],
        out_specs=[seq_spec, _const_spec(s0r.shape), _const_spec(s0i.shape)],
        out_shape=[jax.ShapeDtypeStruct(u.shape, BF16), jax.ShapeDtypeStruct(s0r.shape, F32),
                   jax.ShapeDtypeStruct(s0i.shape, F32)],
        scratch_shapes=[pltpu.VMEM((rows, n_state), F32), pltpu.VMEM((rows, n_state), F32),
                        pltpu.VMEM((nb, n_state), F32), pltpu.VMEM((nb, n_state), F32)],
        compiler_params=_cparams(("arbitrary",)),
        name="s5",
    )(u, s0r, s0i, *consts)


def _attn_prompt_kernel(tq, q_ref, kp_ref, kc_ref, vp_ref, vc_ref, o_ref,
                        acc, mst, lst, mask_t, one_hot, sbuf, pbuf):
    it = pl.program_id(2)
    qb = Q_BLOCK
    scale = HEAD_DIM ** -0.5
    lane = lax.broadcasted_iota(jnp.int32, (1, LANES), 1)
    head0 = lane < HEAD_DIM
    key = lax.broadcasted_iota(jnp.int32, (2 * qb, qb), 0)
    row = lax.broadcasted_iota(jnp.int32, (2 * qb, qb), 1)
    band_t = (key >= row) & (key <= row + qb)
    mask_t[0] = jnp.where(band_t, 0.0, NEG_INF).astype(BF16)
    mask_t[1] = jnp.where(band_t & (key >= qb), 0.0, NEG_INF).astype(BF16)
    one_hot[...] = (key % qb == row).astype(BF16)

    blocks = []
    for g, (w, d) in enumerate(DILATED_GROUPS):
        assert w // d == qb
        nblk = tq // (qb * d)
        for idx in range(d * nblk):
            qstart = idx // nblk + d * qb * (idx % nblk)
            kstart = qstart - d * qb
            stride = {} if d == 1 else {"stride": d}
            if kstart < 0:
                ks = (pl.ds(tq + kstart, qb, **stride), pl.ds(qstart, qb, **stride))
            else:
                ks = (pl.ds(kstart, 2 * qb, **stride),)
            blocks.append((g, kstart < 0, pl.ds(qstart, qb, **stride), ks))

    def key_rows(prev_ref, cur_ref, ks):
        if len(ks) == 1:
            return cur_ref[ks[0], :].astype(BF16)
        return jnp.concatenate([prev_ref[ks[0], :], cur_ref[ks[1], :]], axis=0).astype(BF16)

    n_slot = sbuf.shape[0]
    first_tile = (it == 0).astype(jnp.int32)

    def scores(j):
        _, before_tile, qs, ks = blocks[j]
        qt = q_ref[qs, :] * (scale * LOG2E)
        q2 = jnp.concatenate([jnp.where(head0, qt, 0.0), jnp.where(head0, 0.0, qt)], axis=0).astype(BF16)
        q_aug = jnp.concatenate([q2, one_hot[...]], axis=1)
        k_aug = jnp.concatenate([key_rows(kp_ref, kc_ref, ks), mask_t[first_tile] if before_tile else mask_t[0]],
                                axis=1)
        sbuf[j % n_slot] = lax.dot_general(q_aug, k_aug, (((1,), (1,)), ((), ())), preferred_element_type=F32)

    def probs(j):
        g, _, qs, _ = blocks[j]
        sl = j % n_slot
        m = jnp.max(jnp.maximum(sbuf[sl, :, 0:LANES], sbuf[sl, :, LANES:2 * LANES]), axis=-1, keepdims=True)
        pbuf[sl, :, 0:LANES] = jnp.exp2(sbuf[sl, :, 0:LANES] - m).astype(BF16)
        pbuf[sl, :, LANES:2 * LANES] = jnp.exp2(sbuf[sl, :, LANES:2 * LANES] - m).astype(BF16)
        mst[g, qs, :] = jnp.where(head0, m[0:qb], m[qb:2 * qb])

    def values(j):
        g, _, qs, ks = blocks[j]
        v_aug = jnp.concatenate([key_rows(vp_ref, vc_ref, ks), jnp.ones((2 * qb, LANES), BF16)], axis=1)
        pv = _dot(pbuf[j % n_slot], v_aug)
        acc[g, qs, :] = jnp.where(head0, pv[0:qb, 0:LANES], pv[qb:2 * qb, 0:LANES])
        lst[g, qs, :] = jnp.where(head0, pv[0:qb, LANES:2 * LANES], pv[qb:2 * qb, LANES:2 * LANES])

    group = ATTN_GROUP
    n_chunk = len(blocks) // group
    for step in range(n_chunk + 2):
        @pl.when(it >= 0)
        def _(step=step):
            for c, stage in ((step - 2, values), (step - 1, probs), (step, scores)):
                if 0 <= c < n_chunk:
                    for j in range(c * group, (c + 1) * group):
                        stage(j)

    n_g = len(DILATED_GROUPS)
    m_top = functools.reduce(jnp.maximum, [mst[g] for g in range(n_g)])
    num = 0.0
    den = 0.0
    for g in range(n_g):
        wgt = jnp.exp2(mst[g] - m_top)
        num = num + wgt * acc[g]
        den = den + wgt * lst[g]
    o_ref[...] = (num / den).astype(o_ref.dtype)


def _attn_prompt(q, k, v):
    B, T, W = q.shape
    tq = W_MAX
    assert T % tq == 0 and W % LANES == 0
    n_hp = W // LANES
    cur = pl.BlockSpec((None, tq, LANES), lambda b, h, i: (b, i, h))
    prev = pl.BlockSpec((None, tq, LANES), lambda b, h, i: (b, jnp.maximum(i - 1, 0), h))
    n_g = len(DILATED_GROUPS)
    n_blk, qb2 = ATTN_SLOTS, 2 * Q_BLOCK
    return pl.pallas_call(
        functools.partial(_attn_prompt_kernel, tq),
        grid=(B, n_hp, T // tq),
        in_specs=[cur, prev, cur, prev, cur],
        out_specs=cur,
        out_shape=jax.ShapeDtypeStruct((B, T, W), BF16),
        scratch_shapes=[pltpu.VMEM((n_g, tq, LANES), F32), pltpu.VMEM((n_g, tq, LANES), F32),
                        pltpu.VMEM((n_g, tq, LANES), F32),
                        pltpu.VMEM((2, qb2, Q_BLOCK), BF16), pltpu.VMEM((qb2, Q_BLOCK), BF16),
                        pltpu.VMEM((n_blk, qb2, qb2), F32), pltpu.VMEM((n_blk, qb2, qb2), BF16)],
        compiler_params=_cparams(("parallel", "parallel", "arbitrary")),
        name="attn_prompt",
    )(q, k, k, v, v)


def _attn_sample_kernel(n_new, q_ref, kc_ref, vc_ref, kn_ref, vn_ref, o_ref):
    P = kc_ref.shape[0]
    W = kc_ref.shape[1]
    pad = q_ref.shape[0]
    n_rows = ATTN_HEADS * pad
    scale = HEAD_DIM ** -0.5
    q = q_ref[...]
    col_head = lax.broadcasted_iota(jnp.int32, (pad, W), 1) // HEAD_DIM
    qbd = jnp.concatenate([jnp.where(col_head == h, q, 0.0) for h in range(ATTN_HEADS)], axis=0).astype(BF16)

    def multiplicity(dist):
        cnt = jnp.zeros(dist.shape, F32)
        for w, d in DILATED_GROUPS:
            cnt = cnt + ((dist >= 0) & (dist <= w) & (dist % d == 0)).astype(F32)
        return cnt

    nt = (((1,), (1,)), ((), ()))
    s_c = lax.dot_general(qbd, kc_ref[...].astype(BF16), nt, preferred_element_type=F32) * scale
    s_n = lax.dot_general(qbd, kn_ref[...].astype(BF16), nt, preferred_element_type=F32) * scale
    j_c = lax.broadcasted_iota(jnp.int32, (n_rows, P), 0) % pad
    i_c = lax.broadcasted_iota(jnp.int32, (n_rows, P), 1)
    mult_c = multiplicity(j_c + P - i_c)
    j_n = lax.broadcasted_iota(jnp.int32, (n_rows, pad), 0) % pad
    i_n = lax.broadcasted_iota(jnp.int32, (n_rows, pad), 1)
    mult_n = jnp.where(i_n < n_new, multiplicity(j_n - i_n), 0.0)
    s_c = jnp.where(mult_c > 0, s_c, NEG_INF)
    s_n = jnp.where(mult_n > 0, s_n, NEG_INF)
    m = jnp.maximum(jnp.max(s_c, axis=-1, keepdims=True), jnp.max(s_n, axis=-1, keepdims=True))
    p_c = mult_c * jnp.exp(s_c - m)
    p_n = mult_n * jnp.exp(s_n - m)
    l = jnp.sum(p_c, axis=-1, keepdims=True) + jnp.sum(p_n, axis=-1, keepdims=True)
    pv = (_dot(p_c.astype(BF16), vc_ref[...].astype(BF16)) + _dot(p_n.astype(BF16), vn_ref[...].astype(BF16))) / l
    out = jnp.zeros((pad, W), F32)
    for h in range(ATTN_HEADS):
        out = out + jnp.where(col_head == h, pv[h * pad:(h + 1) * pad], 0.0)
    o_ref[...] = out


def _attn_sample(q, k_new, v_new, k_cache, v_cache, layer, n_new):
    B, pad, W = q.shape
    P = k_cache.shape[2]
    small = pl.BlockSpec((None, pad, W), lambda b: (b, 0, 0))
    big = pl.BlockSpec((None, None, P, W), lambda b: (layer, b, 0, 0))
    return pl.pallas_call(
        functools.partial(_attn_sample_kernel, n_new),
        grid=(B,),
        in_specs=[small, big, big, small, small],
        out_specs=small,
        out_shape=jax.ShapeDtypeStruct((B, pad, W), F32),
        compiler_params=_cparams(("parallel",)),
        name="attn_sample",
    )(q, k_cache, v_cache, k_new, v_new)


def _mix_mlp_kernel(ff_chunk, x_ref, gpre_ref, wg_ref, ya_ref, yb_ref, yc_ref, wa_ref, wb_ref, wc_ref, wo_ref,
                    gpost_ref, g2pre_ref, w1_ref, w2_ref, g2post_ref, o_ref):
    x = x_ref[...]
    D = x.shape[1]
    h = _rms(x, gpre_ref[...]).astype(BF16)
    merged = None
    for i, (y_ref, w_ref) in enumerate(((ya_ref, wa_ref), (yb_ref, wb_ref), (yc_ref, wc_ref))):
        gate = jax.nn.sigmoid(_dot(h, wg_ref[:, i * D:(i + 1) * D]))
        term = gate * _dot(y_ref[...].astype(BF16), w_ref[...])
        merged = term if merged is None else merged + term
    x = x + _rms(_dot(merged.astype(BF16), wo_ref[...]), gpost_ref[...])

    h = _rms(x, g2pre_ref[...]).astype(BF16)
    acc = None
    for c0 in range(0, w1_ref.shape[1], ff_chunk):
        hid = jnp.square(jnp.maximum(_dot(h, w1_ref[:, c0:c0 + ff_chunk]), 0.0)).astype(BF16)
        part = _dot(hid, w2_ref[c0:c0 + ff_chunk, :])
        acc = part if acc is None else acc + part
    o_ref[...] = x + _rms(acc, g2post_ref[...])


def _mix_mlp(x, ya, yb, yc, p, rows):
    B, T, D = x.shape
    spec = lambda a: pl.BlockSpec((None, rows, a.shape[2]), lambda b, i: (b, i, 0))
    consts = [p['norm_pre_mix'], p['w_gates']]
    tail = [p['w_proj_a'], p['w_proj_b'], p['w_proj_c'], p['w_out'], p['norm_post_mix'],
            p['norm_pre_mlp'], p['mlp_w1'], p['mlp_w2'], p['norm_post_mlp']]
    return pl.pallas_call(
        functools.partial(_mix_mlp_kernel, FF_CHUNK),
        grid=(B, T // rows),
        in_specs=[spec(x)] + [_resident_spec(a.shape) for a in consts] + [spec(ya), spec(yb), spec(yc)]
                 + [_resident_spec(a.shape) for a in tail],
        out_specs=spec(x),
        out_shape=jax.ShapeDtypeStruct((B, T, D), F32),
        compiler_params=_cparams(("parallel", "parallel")),
        name="mix_mlp",
    )(x, *consts, ya, yb, yc, *tail)


def _block_diag_tiles(w, per_tile):
    n, c, d = w.shape
    w = w.reshape(n // per_tile, per_tile, c, d)
    eye = jnp.eye(per_tile, dtype=w.dtype)
    return jnp.einsum('jpcd,pq->jpcqd', w, eye).reshape(n // per_tile, per_tile * c, per_tile * d)


def _layer_params(l, a):
    D = a['w_in'].shape[1]
    lru_w = a['conv_w'].shape[2]
    n_blk, blk = a['lru_wr'].shape[1], a['lru_wr'].shape[2]
    G, N, GC = a['ssm_b_re'].shape[1:]
    ssm_w = G * GC
    attn_w = ATTN_HEADS * HEAD_DIM
    n_front = lru_w + ssm_w + 3 * attn_w
    row = lambda v: v.reshape(1, -1).astype(F32)
    p = {
        'norm_pre_mix': row(a['norm_pre_mix'][l]), 'norm_post_mix': row(a['norm_post_mix'][l]),
        'norm_pre_mlp': row(a['norm_pre_mlp'][l]), 'norm_post_mlp': row(a['norm_post_mlp'][l]),
        'w_front': a['w_in_bf16'][l][:, :n_front], 'w_gates': a['w_in_bf16'][l][:, n_front:],
        'w_proj_a': a['w_proj_a'][l].astype(BF16), 'w_proj_b': a['w_proj_b'][l].astype(BF16),
        'w_proj_c': a['w_proj_c'][l].astype(BF16), 'w_out': a['w_out'][l].astype(BF16),
        'mlp_w1': a['mlp_w1'][l].astype(BF16), 'mlp_w2': a['mlp_w2'][l].astype(BF16),
        'widths': (lru_w, ssm_w, attn_w, attn_w, attn_w),
    }
    per_tile = MXU_TILE // blk
    p['lru'] = {
        'conv_w': a['conv_w'][l].astype(F32), 'conv_b': row(a['conv_b'][l]),
        'wr': _block_diag_tiles(a['lru_wr'][l], per_tile).astype(BF16), 'br': row(a['lru_br'][l]),
        'wi': _block_diag_tiles(a['lru_wi'][l], per_tile).astype(BF16), 'bi': row(a['lru_bi'][l]),
        'lam': row(a['lru_lambda'][l]),
    }
    lr, li = a['ssm_lambda_re'][l].astype(F32), a['ssm_lambda_im'][l].astype(F32)
    dt = jnp.exp(a['ssm_log_dt'][l].astype(F32))[:, None]
    mag = jnp.exp(lr * dt)
    ang = li * dt
    ab_re, ab_im = mag * jnp.cos(ang), mag * jnp.sin(ang)
    den = lr * lr + li * li
    f_re = ((ab_re - 1.0) * lr + ab_im * li) / den
    f_im = (ab_im * lr - (ab_re - 1.0) * li) / den
    b_re, b_im = a['ssm_b_re'][l].astype(F32), a['ssm_b_im'][l].astype(F32)
    bb_re = f_re[..., None] * b_re - f_im[..., None] * b_im
    bb_im = f_re[..., None] * b_im + f_im[..., None] * b_re
    g_half = G // 2
    to_b = lambda m: _block_diag_tiles(jnp.swapaxes(m, 1, 2), g_half).astype(BF16)
    to_c = lambda m: _block_diag_tiles(jnp.swapaxes(m, 1, 2), g_half).astype(BF16)
    p['s5'] = {
        'a_re': ab_re.reshape(1, G * N), 'a_im': ab_im.reshape(1, G * N),
        'b_re': to_b(bb_re), 'b_im': to_b(bb_im),
        'c_re': to_c(a['ssm_c_re'][l].astype(F32)), 'c_im': to_c(a['ssm_c_im'][l].astype(F32)),
        'd': row(a['ssm_d'][l]), 'glu_w': a['glu_w'][l].astype(BF16), 'glu_b': row(a['glu_b'][l]),
    }
    return p


def _tm(state):
    nb, k, w = state.shape
    return jnp.swapaxes(state, 0, 1).reshape(k * nb, w)


def _layer_prompt(x, p, rows, tt, keep, layer, depth, tails):
    B, T, D = x.shape
    lru_w, ssm_w, attn_w = p['widths'][0], p['widths'][1], p['widths'][2]
    xa, u, q, k, v, k_tail, v_tail = _in_proj(x, p['norm_pre_mix'], p['w_front'], p['widths'], rows,
                                              keep, layer, depth, tails)
    n_state = p['s5']['a_re'].shape[1]
    halo = (p['lru']['conv_w'].shape[0] - 1) * B
    ya, nconv, hlast = _rglru(xa, jnp.zeros((halo, lru_w), F32), jnp.zeros((B, lru_w), F32), p['lru'], B, tt)
    yb, s_re, s_im = _s5(u, jnp.zeros((B, n_state), F32), jnp.zeros((B, n_state), F32), p['s5'], B, tt)
    yc = _attn_prompt(q, k, v)
    x = _mix_mlp(x, ya, yb, yc, p, rows)
    return x, nconv, hlast, s_re, s_im, k_tail, v_tail


def _layer_sample(x, conv_state, h0, s0r, s0i, k_cache, v_cache, layer, p, nb, n_new):
    rows = x.shape[1]
    lru_w, ssm_w, attn_w = p['widths'][0], p['widths'][1], p['widths'][2]
    xa, u, q, k, v = _in_proj(x, p['norm_pre_mix'], p['w_front'], p['widths'], rows)
    ya, nconv, hlast = _rglru(xa[0], conv_state, h0, p['lru'], nb, n_new)
    yb, s_re, s_im = _s5(u[0], s0r, s0i, p['s5'], nb, n_new)
    pad = 8

    def bm(arr):
        arr = jnp.swapaxes(arr.reshape(n_new, nb, attn_w), 0, 1)
        return jnp.pad(arr, ((0, 0), (0, pad - n_new), (0, 0)))

    yc = _attn_sample(bm(q), bm(k), bm(v), k_cache, v_cache, layer, n_new)[:, :n_new]
    yc = jnp.swapaxes(yc, 0, 1).reshape(1, rows, attn_w)
    x = _mix_mlp(x, ya[None], yb[None], yc, p, rows)
    return x, nconv, hlast, s_re, s_im, k, v


def kernel(x_prompt, x_sample, state_conv, state_lru, state_ssm_re, state_ssm_im, cache_k, cache_v, norm_pre_mix, norm_post_mix, norm_pre_mlp, norm_post_mlp, w_in, conv_w, conv_b, lru_wr, lru_br, lru_wi, lru_bi, lru_lambda, ssm_lambda_re, ssm_lambda_im, ssm_log_dt, ssm_b_re, ssm_b_im, ssm_c_re, ssm_c_im, ssm_d, glu_w, glu_b, w_proj_a, w_proj_b, w_proj_c, w_out, mlp_w1, mlp_w2):
    a = dict(norm_pre_mix=norm_pre_mix, norm_post_mix=norm_post_mix, norm_pre_mlp=norm_pre_mlp,
             norm_post_mlp=norm_post_mlp, w_in=w_in, conv_w=conv_w, conv_b=conv_b, lru_wr=lru_wr, lru_br=lru_br,
             lru_wi=lru_wi, lru_bi=lru_bi, lru_lambda=lru_lambda, ssm_lambda_re=ssm_lambda_re,
             ssm_lambda_im=ssm_lambda_im, ssm_log_dt=ssm_log_dt, ssm_b_re=ssm_b_re, ssm_b_im=ssm_b_im,
             ssm_c_re=ssm_c_re, ssm_c_im=ssm_c_im, ssm_d=ssm_d, glu_w=glu_w, glu_b=glu_b, w_proj_a=w_proj_a,
             w_proj_b=w_proj_b, w_proj_c=w_proj_c, w_out=w_out, mlp_w1=mlp_w1, mlp_w2=mlp_w2,
             w_in_bf16=w_in.astype(BF16))
    depth = w_in.shape[0]
    Bp, Tp, D = x_prompt.shape
    Bs, Ts, _ = x_sample.shape
    G, N = state_ssm_re.shape[2:]
    attn_w = ATTN_HEADS * HEAD_DIM
    P = cache_k.shape[2]
    assert P == W_MAX, "the window buffer must hold exactly the widest window"
    keep = min(W_MAX, Tp)
    conv_k = state_conv.shape[2]

    xp = x_prompt
    xs = jnp.swapaxes(x_sample, 0, 1).reshape(1, Ts * Bs, D)
    k_rows = cache_k.reshape(depth, Bs, P, attn_w)
    v_rows = cache_v.reshape(depth, Bs, P, attn_w)
    outs = [[] for _ in range(12)]
    tails = ()
    for l in range(depth):
        p = _layer_params(l, a)
        xp, c_, h_, r_, i_, pk, pv = _layer_prompt(xp, p, rows=ROW_BLOCK, tt=TIME_BLOCK, keep=keep, layer=l, depth=depth,
                                                   tails=tails)
        tails = (pk, pv)
        for lst, val in zip(outs[:4], (jnp.swapaxes(c_.reshape(conv_k, Bp, -1), 0, 1), h_,
                                       r_.reshape(Bp, G, N), i_.reshape(Bp, G, N))):
            lst.append(val)
        xs, c_, h_, r_, i_, k_, v_ = _layer_sample(
            xs, _tm(state_conv[l]), state_lru[l], state_ssm_re[l].reshape(Bs, G * N),
            state_ssm_im[l].reshape(Bs, G * N), k_rows, v_rows, l, p, Bs, Ts)
        sk = jnp.swapaxes(k_.reshape(Ts, Bs, ATTN_HEADS, HEAD_DIM), 0, 1)
        sv = jnp.swapaxes(v_.reshape(Ts, Bs, ATTN_HEADS, HEAD_DIM), 0, 1)
        for lst, val in zip(outs[6:], (jnp.swapaxes(c_.reshape(conv_k, Bs, -1), 0, 1), h_,
                                       r_.reshape(Bs, G, N), i_.reshape(Bs, G, N), sk, sv)):
            lst.append(val)
    ys = jnp.swapaxes(xs.reshape(Ts, Bs, D), 0, 1)
    stacked = [jnp.stack(o) if o else None for o in outs]
    stacked[4], stacked[5] = tails
    return (xp, ys) + tuple(stacked)
```
